```python
import math
import jax
import jax.numpy as jnp
from jax import lax
import numpy as np


D_MODEL = 2048
BATCH = 1
SEQ = 8192
DEPTH = 4

CHUNK = 64
Q_BLOCK = 128
HEAD_DIM = 128
N_HEADS = D_MODEL // HEAD_DIM
H_SB = N_HEADS // 3
H_FOX = N_HEADS // 3
H_DIFF = N_HEADS - H_SB - H_FOX
MIX_WIDTH = N_HEADS * HEAD_DIM
DIFF_HALF = HEAD_DIM // 2
N_IN = 3 * MIX_WIDTH + H_FOX
N_EXPERTS = 16
N_GROUPS = 4
EXPERTS_PER_GROUP = N_EXPERTS // N_GROUPS
TOP_K = 2
D_EXPERT = D_MODEL // 4
EPS = 1e-6

kernel_name = 'hybrid_diff_sb_fox_groupmoe_trunk'


def _split_points():
    wa, wb, wc = H_DIFF * HEAD_DIM, H_SB * HEAD_DIM, H_FOX * HEAD_DIM
    widths = [wa, wa, wa, wb, wb, wb, wc, wc, wc]
    pts, acc = [], 0
    for w in widths:
        acc += w
        pts.append(acc)
    return pts


def rms_norm(x, g):
    xf = x.astype(jnp.float32)
    y = xf * lax.rsqrt(jnp.mean(xf * xf, axis=-1, keepdims=True) + EPS)
    return (y * g.astype(jnp.float32)).astype(x.dtype)


def to_heads(t, n_heads, d):
    b, s, _ = t.shape
    return t.reshape(b, s, n_heads, d).transpose(0, 2, 1, 3)


def from_heads(t):
    b, h, s, d = t.shape
    return t.transpose(0, 2, 1, 3).reshape(b, s, h * d)


def _to_blocks(t):
    b, h, s = t.shape[:3]
    t = t.reshape((b, h, s // Q_BLOCK, Q_BLOCK) + t.shape[3:])
    return jnp.moveaxis(t, 2, 0)


def _from_blocks(o):
    o = jnp.moveaxis(o, 0, 2)
    b, h, nb, qb, d = o.shape
    return o.reshape(b, h, nb * qb, d)


def diff_attention(q1, q2, k1, k2, v, lam, slopes):
    seq = k1.shape[2]
    kpos = jnp.arange(seq)
    scale = DIFF_HALF ** -0.5

    def block(args):
        q1b, q2b, start = args
        qpos = start + jnp.arange(Q_BLOCK)
        allowed = (kpos[None, :] // CHUNK) <= (qpos[:, None] // CHUNK)
        dist = jnp.abs(qpos[:, None] - kpos[None, :]).astype(jnp.float32)
        bias = jnp.where(allowed[None], -slopes[:, None, None] * dist[None], -jnp.inf)
        s1 = jnp.einsum('bhqd,bhkd->bhqk', q1b, k1).astype(jnp.float32) * scale + bias
        s2 = jnp.einsum('bhqd,bhkd->bhqk', q2b, k2).astype(jnp.float32) * scale + bias
        w = jax.nn.softmax(s1, axis=-1) - lam * jax.nn.softmax(s2, axis=-1)
        return jnp.einsum('bhqk,bhkd->bhqd', w.astype(v.dtype), v)

    starts = jnp.arange(seq // Q_BLOCK) * Q_BLOCK
    return _from_blocks(lax.map(block, (_to_blocks(q1), _to_blocks(q2), starts)))


def stick_breaking_attention(q, k, v):
    seq = k.shape[2]
    kpos = jnp.arange(seq)
    scale = HEAD_DIM ** -0.5

    def block(args):
        qb, start = args
        qpos = start + jnp.arange(Q_BLOCK)
        strict = kpos[None, :] < qpos[:, None]
        z = jnp.einsum('bhqd,bhkd->bhqk', qb, k).astype(jnp.float32) * scale
        log_keep = jnp.where(strict, jax.nn.log_sigmoid(-z), 0.0)
        between = lax.cumsum(log_keep, axis=3, reverse=True) - log_keep
        a = jnp.where(strict, jnp.exp(jax.nn.log_sigmoid(z) + between), 0.0)
        return jnp.einsum('bhqk,bhkd->bhqd', a.astype(v.dtype), v)

    starts = jnp.arange(seq // Q_BLOCK) * Q_BLOCK
    return _from_blocks(lax.map(block, (_to_blocks(q), starts)))


def forgetting_attention(q, k, v, dcum):
    seq = k.shape[2]
    kpos = jnp.arange(seq)
    scale = HEAD_DIM ** -0.5

    def block(args):
        qb, dq, start = args
        qpos = start + jnp.arange(Q_BLOCK)
        causal = kpos[None, :] <= qpos[:, None]
        logits = jnp.einsum('bhqd,bhkd->bhqk', qb, k).astype(jnp.float32) * scale
        logits = logits + dq[..., :, None] - dcum[:, :, None, :]
        logits = jnp.where(causal, logits, -jnp.inf)
        p = jax.nn.softmax(logits, axis=-1)
        return jnp.einsum('bhqk,bhkd->bhqd', p.astype(v.dtype), v)

    starts = jnp.arange(seq // Q_BLOCK) * Q_BLOCK
    return _from_blocks(lax.map(block, (_to_blocks(q), _to_blocks(dcum), starts)))


def moe_ffn(h, w_router, router_bias, wg, wu, wd):
    b, s, d = h.shape
    t = h.reshape(b * s, d)
    scores = jax.nn.sigmoid((t @ w_router).astype(jnp.float32))
    sel = scores + router_bias.astype(jnp.float32)
    grp = sel.reshape(-1, N_GROUPS, EXPERTS_PER_GROUP)
    group_score = lax.top_k(grp, TOP_K)[0].sum(-1)
    best_group = jnp.argmax(group_score, axis=-1)
    in_group = (jnp.arange(N_EXPERTS) // EXPERTS_PER_GROUP)[None, :] == best_group[:, None]
    _, idx = lax.top_k(jnp.where(in_group, sel, -jnp.inf), TOP_K)
    gate = jnp.take_along_axis(scores, idx, axis=-1)
    gate = gate / jnp.sum(gate, axis=-1, keepdims=True)
    combine = jnp.sum(jax.nn.one_hot(idx, N_EXPERTS, dtype=jnp.float32) * gate[..., None], axis=1)
    hg = jnp.einsum('nd,edf->nef', t, wg)
    hu = jnp.einsum('nd,edf->nef', t, wu)
    act = jax.nn.silu(hg) * hu * combine[:, :, None].astype(t.dtype)
    out = jnp.einsum('nef,efd->nd', act, wd)
    return out.reshape(b, s, d)


def setup_inputs(seed: int = 0) -> dict:
    key = jax.random.key(seed)
    ks = jax.random.split(key, 24)
    f32 = jnp.float32
    nrm = lambda k, shape, sc: jax.random.normal(k, shape, f32) * sc
    gain = lambda k, shape: 1.0 + 0.02 * jax.random.normal(k, shape, f32)
    res_scale = (2.0 * DEPTH) ** -0.5
    return {
        'x': jax.random.normal(ks[0], (BATCH, SEQ, D_MODEL), f32),
        'attn_norm': gain(ks[1], (DEPTH, D_MODEL)),
        'w_in': nrm(ks[2], (DEPTH, D_MODEL, N_IN), D_MODEL ** -0.5),
        'b_forget': jax.random.uniform(ks[3], (DEPTH, H_FOX), f32, 1.0, 6.0),
        'qnorm_diff': gain(ks[4], (DEPTH, DIFF_HALF)),
        'knorm_diff': gain(ks[5], (DEPTH, DIFF_HALF)),
        'lam_q1': nrm(ks[6], (DEPTH, DIFF_HALF), 0.1),
        'lam_k1': nrm(ks[7], (DEPTH, DIFF_HALF), 0.1),
        'lam_q2': nrm(ks[8], (DEPTH, DIFF_HALF), 0.1),
        'lam_k2': nrm(ks[9], (DEPTH, DIFF_HALF), 0.1),
        'subln_diff': gain(ks[10], (DEPTH, HEAD_DIM)),
        'onorm_sb': gain(ks[11], (DEPTH, HEAD_DIM)),
        'qnorm_fox': gain(ks[12], (DEPTH, HEAD_DIM)),
        'knorm_fox': gain(ks[13], (DEPTH, HEAD_DIM)),
        'onorm_fox': gain(ks[14], (DEPTH, HEAD_DIM)),
        'w_out': nrm(ks[15], (DEPTH, MIX_WIDTH, D_MODEL), MIX_WIDTH ** -0.5 * res_scale),
        'ffn_norm': gain(ks[16], (DEPTH, D_MODEL)),
        'w_router': nrm(ks[17], (D_MODEL, N_EXPERTS), D_MODEL ** -0.5),
        'router_bias': nrm(ks[18], (N_EXPERTS,), 0.01),
        'w_gate': nrm(ks[19], (DEPTH, N_EXPERTS, D_MODEL, D_EXPERT), D_MODEL ** -0.5),
        'w_up': nrm(ks[20], (DEPTH, N_EXPERTS, D_MODEL, D_EXPERT), D_MODEL ** -0.5),
        'w_down': nrm(ks[21], (DEPTH, N_EXPERTS, D_EXPERT, D_MODEL), D_EXPERT ** -0.5 * res_scale),
    }


def reference(x, attn_norm, w_in, b_forget, qnorm_diff, knorm_diff, lam_q1, lam_k1, lam_q2, lam_k2,
              subln_diff, onorm_sb, qnorm_fox, knorm_fox, onorm_fox, w_out, ffn_norm,
              w_router, router_bias, w_gate, w_up, w_down):
    b, s, _ = x.shape
    slopes = jnp.exp2(-8.0 * jnp.arange(1, H_DIFF + 1, dtype=jnp.float32) / H_DIFF)
    pts = _split_points()
    for l in range(DEPTH):
        xn = rms_norm(x, attn_norm[l])
        proj = xn @ w_in[l]
        qa, ka, va, qb, kb, vb, qc, kc, vc, fc = jnp.split(proj, pts, axis=-1)

        qa = rms_norm(qa.reshape(b, s, H_DIFF, 2, DIFF_HALF).transpose(0, 2, 1, 3, 4), qnorm_diff[l])
        ka = rms_norm(ka.reshape(b, s, H_DIFF, 2, DIFF_HALF).transpose(0, 2, 1, 3, 4), knorm_diff[l])
        lam_init = 0.8 - 0.6 * math.exp(-0.3 * l)
        lam = (jnp.exp(jnp.sum(lam_q1[l].astype(jnp.float32) * lam_k1[l].astype(jnp.float32)))
               - jnp.exp(jnp.sum(lam_q2[l].astype(jnp.float32) * lam_k2[l].astype(jnp.float32)))
               + lam_init)
        oa = diff_attention(qa[..., 0, :], qa[..., 1, :], ka[..., 0, :], ka[..., 1, :],
                            to_heads(va, H_DIFF, HEAD_DIM), lam, slopes)
        oa = rms_norm(oa, subln_diff[l]) * (1.0 - lam_init)

        ob = stick_breaking_attention(to_heads(qb, H_SB, HEAD_DIM), to_heads(kb, H_SB, HEAD_DIM),
                                      to_heads(vb, H_SB, HEAD_DIM))
        ob = rms_norm(ob, onorm_sb[l])

        log_f = jax.nn.log_sigmoid((fc + b_forget[l]).astype(jnp.float32))
        dcum = jnp.cumsum(log_f, axis=1).transpose(0, 2, 1)
        oc = forgetting_attention(rms_norm(to_heads(qc, H_FOX, HEAD_DIM), qnorm_fox[l]),
                                  rms_norm(to_heads(kc, H_FOX, HEAD_DIM), knorm_fox[l]),
                                  to_heads(vc, H_FOX, HEAD_DIM), dcum)
        oc = rms_norm(oc, onorm_fox[l])

        mixed = jnp.concatenate([from_heads(oa), from_heads(ob), from_heads(oc)], axis=-1)
        x = x + mixed @ w_out[l]
        x = x + moe_ffn(rms_norm(x, ffn_norm[l]), w_router, router_bias, w_gate[l], w_up[l], w_down[l])
    return x
```

```python
import functools
import math

import jax
import jax.numpy as jnp
from jax import lax
from jax.experimental import pallas as pl
from jax.experimental.pallas import tpu as pltpu

F32 = jnp.float32
BF16 = jnp.bfloat16

D_MODEL = 2048
HEAD_DIM = 128
LANE = 128
N_HEADS = 16
H_SB = N_HEADS // 3
H_FOX = N_HEADS // 3
H_DIFF = N_HEADS - H_SB - H_FOX
DIFF_HALF = HEAD_DIM // 2
CHUNK = 64
N_EXPERTS = 16
N_GROUPS = 4
EXPERTS_PER_GROUP = N_EXPERTS // N_GROUPS
D_EXPERT = D_MODEL // 4
EPS = 1e-6
N_MAIN = 3 * N_HEADS * HEAD_DIM
N_IN = N_MAIN + H_FOX

QA, KA, VA = 0, H_DIFF, 2 * H_DIFF
QB = 3 * H_DIFF
KB, VB = QB + H_SB, QB + 2 * H_SB
QC = QB + 3 * H_SB
KC, VC = QC + H_FOX, QC + 2 * H_FOX

VMEM_LIMIT = 48 * 1024 * 1024

_NT = (((1,), (1,)), ((), ()))


def _dot(a, b):
    return jnp.dot(a, b, preferred_element_type=F32)


def _dot_nt(a, b):
    return lax.dot_general(a, b, _NT, preferred_element_type=F32)


def _log_sigmoid(x):
    return jnp.minimum(x, 0.0) - jnp.log1p(jnp.exp(-jnp.abs(x)))


def _split_bf16(x):
    hi = x.astype(BF16)
    lo = (x - hi.astype(F32)).astype(BF16)
    return hi, lo


def _proj_kernel(x_ref, g_ref, w_ref, wfc_ref, bfc_ref, gmat_ref, flag_ref, gain_ref,
                 o_ref, dcum_ref, xn_scr, carry_scr, *, tm, tn):
    i = pl.program_id(0)
    j = pl.program_id(1)

    @pl.when(j == 0)
    def _():
        x = x_ref[...]
        ms = jnp.mean(x * x, axis=-1, keepdims=True)
        xn = (x * lax.rsqrt(ms + EPS) * g_ref[...]).astype(BF16)
        xn_scr[...] = xn

        @pl.when(i == 0)
        def _():
            carry_scr[...] = jnp.zeros_like(carry_scr)

        lf = _log_sigmoid(_dot_nt(wfc_ref[...], xn) + bfc_ref[...])
        hi, lo = _split_bf16(lf)
        r = lax.broadcasted_iota(jnp.int32, (tm, tm), 0)
        c = lax.broadcasted_iota(jnp.int32, (tm, tm), 1)
        tri = (r <= c).astype(BF16)
        cum = _dot(hi, tri) + _dot(lo, tri) + carry_scr[...]
        carry_scr[...] = cum[:, tm - 1:tm]
        dcum_ref[...] = cum

    acc = _dot(xn_scr[...], w_ref[...])
    for c in range(tn // LANE):
        sl = slice(c * LANE, (c + 1) * LANE)
        y = acc[:, sl]
        ss = _dot((y * y).astype(BF16), gmat_ref[c])
        flag = flag_ref[:, sl]
        inv = flag * lax.rsqrt(ss + EPS) + (1.0 - flag)
        o_ref[:, sl] = (y * inv * gain_ref[:, sl]).astype(BF16)


def _proj(x, g, w, wfc_t, bfc, gmat, flag, gain, *, tm, tn):
    s = x.shape[0]
    kern = functools.partial(_proj_kernel, tm=tm, tn=tn)
    return pl.pallas_call(
        kern,
        grid=(s // tm, N_MAIN // tn),
        in_specs=[
            pl.BlockSpec((tm, D_MODEL), lambda i, j: (i, 0)),
            pl.BlockSpec((1, D_MODEL), lambda i, j: (0, 0)),
            pl.BlockSpec((D_MODEL, tn), lambda i, j: (0, j)),
            pl.BlockSpec((LANE, D_MODEL), lambda i, j: (0, 0)),
            pl.BlockSpec((LANE, 1), lambda i, j: (0, 0)),
            pl.BlockSpec((tn // LANE, LANE, LANE), lambda i, j: (j, 0, 0)),
            pl.BlockSpec((1, tn), lambda i, j: (0, j)),
            pl.BlockSpec((1, tn), lambda i, j: (0, j)),
        ],
        out_specs=[
            pl.BlockSpec((tm, tn), lambda i, j: (i, j)),
            pl.BlockSpec((LANE, tm), lambda i, j: (0, i)),
        ],
        out_shape=[
            jax.ShapeDtypeStruct((s, N_MAIN), BF16),
            jax.ShapeDtypeStruct((LANE, s), F32),
        ],
        scratch_shapes=[pltpu.VMEM((tm, D_MODEL), BF16), pltpu.VMEM((LANE, 1), F32)],
        compiler_params=pltpu.CompilerParams(
            dimension_semantics=("arbitrary", "arbitrary"), vmem_limit_bytes=VMEM_LIMIT),
        name="proj",
    )(x, g, w, wfc_t, bfc, gmat, flag, gain)


def _head_rms(o, gain):
    ms = jnp.mean(o * o, axis=-1, keepdims=True)
    return o * lax.rsqrt(ms + EPS) * gain


def _local_iotas(tq, tk):
    return (lax.broadcasted_iota(jnp.int32, (tq, tk), 0),
            lax.broadcasted_iota(jnp.int32, (tq, tk), 1))


def _softmax_update(s, v, m_scr, l_scr, acc_scr):
    m_old = m_scr[...]
    m_new = jnp.maximum(m_old, jnp.max(s, axis=-1, keepdims=True))
    alpha = jnp.exp(m_old - m_new)
    p = jnp.exp(s - m_new)
    l_scr[...] = alpha * l_scr[...] + jnp.sum(p, axis=-1, keepdims=True)
    acc_scr[...] = alpha * acc_scr[...] + _dot(p.astype(BF16), v)
    m_scr[...] = m_new


def _fox_kernel(q_ref, k_ref, v_ref, dq_ref, dk_ref, g_ref, o_ref, m_scr, l_scr, acc_scr, *, t):
    qi = pl.program_id(1)
    q = q_ref[...]
    dq = dq_ref[...]
    m_scr[...] = jnp.full_like(m_scr, -jnp.inf)
    l_scr[...] = jnp.zeros_like(l_scr)
    acc_scr[...] = jnp.zeros_like(acc_scr)

    def step(j, masked):
        ks = k_ref[pl.ds(j * t, t), :]
        vs = v_ref[pl.ds(j * t, t), :]
        s = _dot_nt(q, ks) + dq - dk_ref[:, pl.ds(j * t, t)]
        if masked:
            r, c = _local_iotas(t, t)
            s = jnp.where(c <= r, s, -jnp.inf)
        _softmax_update(s, vs, m_scr, l_scr, acc_scr)

    def body(j, carry):
        step(j, False)
        return carry

    lax.fori_loop(0, qi, body, 0)
    step(qi, True)
    o = acc_scr[...] / l_scr[...]
    o_ref[...] = _head_rms(o, g_ref[...]).astype(BF16)


def _diff_kernel(slope_ref, lam_ref, q_ref, k_ref, v_ref, g_ref, o_ref,
                 m1_scr, l1_scr, a1_scr, m2_scr, l2_scr, a2_scr, *, t):
    h = pl.program_id(0)
    qi = pl.program_id(1)
    slope = slope_ref[h]
    lam = lam_ref[0]
    q = q_ref[...]
    lane = lax.broadcasted_iota(jnp.int32, q.shape, 1)
    q1 = jnp.where(lane < DIFF_HALF, q, jnp.zeros_like(q))
    q2 = jnp.where(lane >= DIFF_HALF, q, jnp.zeros_like(q))
    for m_scr, l_scr, a_scr in ((m1_scr, l1_scr, a1_scr), (m2_scr, l2_scr, a2_scr)):
        m_scr[...] = jnp.full_like(m_scr, -jnp.inf)
        l_scr[...] = jnp.zeros_like(l_scr)
        a_scr[...] = jnp.zeros_like(a_scr)

    rloc = lax.broadcasted_iota(jnp.int32, (t, 1), 0).astype(F32)
    cloc = lax.broadcasted_iota(jnp.int32, (1, t), 1).astype(F32)

    def step(j, masked):
        ks = k_ref[pl.ds(j * t, t), :]
        vs = v_ref[pl.ds(j * t, t), :]
        if masked:
            r, c = _local_iotas(t, t)
            allowed = (c // CHUNK) <= (r // CHUNK)
            dist = jnp.abs(r - c).astype(F32)
            bias = jnp.where(allowed, -slope * dist, -jnp.inf)
        else:
            off = ((qi - j) * t).astype(F32)
            bias = (-slope * (off + rloc)) + slope * cloc
        _softmax_update(_dot_nt(q1, ks) + bias, vs, m1_scr, l1_scr, a1_scr)
        _softmax_update(_dot_nt(q2, ks) + bias, vs, m2_scr, l2_scr, a2_scr)

    def body(j, carry):
        step(j, False)
        return carry

    lax.fori_loop(0, qi, body, 0)
    step(qi, True)
    o = a1_scr[...] / l1_scr[...] - lam * (a2_scr[...] / l2_scr[...])
    o_ref[...] = _head_rms(o, g_ref[...]).astype(BF16)


def _sb_kernel(q_ref, k_ref, v_ref, g_ref, o_ref, run_scr, acc_scr, *, t, tc):
    qi = pl.program_id(1)
    q = q_ref[...]
    run_scr[...] = jnp.zeros_like(run_scr)
    acc_scr[...] = jnp.zeros_like(acc_scr)
    ur = lax.broadcasted_iota(jnp.int32, (tc, tc), 0)
    uc = lax.broadcasted_iota(jnp.int32, (tc, tc), 1)
    upper = (ur > uc).astype(BF16)
    nchunk = t // tc

    def step(j, masked):
        ks = k_ref[pl.ds(j * t, t), :]
        vs = v_ref[pl.ds(j * t, t), :]
        z = _dot_nt(q, ks)
        log_beta = jnp.minimum(z, 0.0) - jnp.log1p(jnp.exp(-jnp.abs(z)))
        log_keep = log_beta - z
        if masked:
            r, c = _local_iotas(t, t)
            strict = c < r
            log_keep = jnp.where(strict, log_keep, 0.0)
        run = run_scr[...]
        parts = []
        for ci in range(nchunk - 1, -1, -1):
            sl = slice(ci * tc, (ci + 1) * tc)
            lk = log_keep[:, sl]
            hi, lo = _split_bf16(lk)
            between = _dot(hi, upper) + _dot(lo, upper) + run
            parts.append(jnp.exp(log_beta[:, sl] + between))
            run = run + jnp.sum(lk, axis=-1, keepdims=True)
        a = jnp.concatenate(parts[::-1], axis=-1)
        if masked:
            a = jnp.where(strict, a, 0.0)
        acc_scr[...] += _dot(a.astype(BF16), vs)
        run_scr[...] = run

    step(qi, True)

    def body(jj, carry):
        step(qi - 1 - jj, False)
        return carry

    lax.fori_loop(0, qi, body, 0)
    o_ref[...] = _head_rms(acc_scr[...], g_ref[...]).astype(BF16)


def _attn_specs(s, t, qc, kc, vc):
    return [
        pl.BlockSpec((t, LANE), lambda h, i: (i, qc + h)),
        pl.BlockSpec((s, LANE), lambda h, i: (0, kc + h)),
        pl.BlockSpec((s, LANE), lambda h, i: (0, vc + h)),
    ]


_ATTN_PARAMS = pltpu.CompilerParams(
    dimension_semantics=("arbitrary", "arbitrary"), vmem_limit_bytes=VMEM_LIMIT)


def _fox(proj, dq, dk, gain, *, t):
    s = proj.shape[0]
    col = pltpu.VMEM((t, 1), F32)
    return pl.pallas_call(
        functools.partial(_fox_kernel, t=t),
        grid=(H_FOX, s // t),
        in_specs=_attn_specs(s, t, QC, KC, VC) + [
            pl.BlockSpec((None, t, 1), lambda h, i: (h, i, 0)),
            pl.BlockSpec((None, 1, s), lambda h, i: (h, 0, 0)),
            pl.BlockSpec((1, LANE), lambda h, i: (0, 0)),
        ],
        out_specs=pl.BlockSpec((t, LANE), lambda h, i: (i, h)),
        out_shape=jax.ShapeDtypeStruct((s, H_FOX * LANE), BF16),
        scratch_shapes=[col, col, pltpu.VMEM((t, LANE), F32)],
        compiler_params=_ATTN_PARAMS,
        name="fox_attn",
    )(proj, proj, proj, dq, dk, gain)


def _diff(proj, slopes, lam, gain, *, t):
    s = proj.shape[0]
    col = pltpu.VMEM((t, 1), F32)
    acc = pltpu.VMEM((t, LANE), F32)
    smem = pl.BlockSpec(memory_space=pltpu.SMEM)
    return pl.pallas_call(
        functools.partial(_diff_kernel, t=t),
        grid=(H_DIFF, s // t),
        in_specs=[smem, smem] + _attn_specs(s, t, QA, KA, VA) + [
            pl.BlockSpec((1, LANE), lambda h, i: (0, 0)),
        ],
        out_specs=pl.BlockSpec((t, LANE), lambda h, i: (i, h)),
        out_shape=jax.ShapeDtypeStruct((s, H_DIFF * LANE), BF16),
        scratch_shapes=[col, col, acc, col, col, acc],
        compiler_params=_ATTN_PARAMS,
        name="diff_attn",
    )(slopes, lam, proj, proj, proj, gain)


def _sb(proj, gain, *, t, tc):
    s = proj.shape[0]
    return pl.pallas_call(
        functools.partial(_sb_kernel, t=t, tc=tc),
        grid=(H_SB, s // t),
        in_specs=_attn_specs(s, t, QB, KB, VB) + [
            pl.BlockSpec((1, LANE), lambda h, i: (0, 0)),
        ],
        out_specs=pl.BlockSpec((t, LANE), lambda h, i: (i, h)),
        out_shape=jax.ShapeDtypeStruct((s, H_SB * LANE), BF16),
        scratch_shapes=[pltpu.VMEM((t, 1), F32), pltpu.VMEM((t, LANE), F32)],
        compiler_params=_ATTN_PARAMS,
        name="sb_attn",
    )(proj, proj, proj, gain)


def _route(logits_t, bias):
    score = [jax.nn.sigmoid(logits_t[e:e + 1, :]) for e in range(N_EXPERTS)]
    sel = [score[e] + bias[e:e + 1, :] for e in range(N_EXPERTS)]
    gscore = []
    for g in range(N_GROUPS):
        mem = sel[g * EXPERTS_PER_GROUP:(g + 1) * EXPERTS_PER_GROUP]
        best = None
        for a in range(EXPERTS_PER_GROUP):
            for b in range(a + 1, EXPERTS_PER_GROUP):
                pair = mem[a] + mem[b]
                best = pair if best is None else jnp.maximum(best, pair)
        gscore.append(best)
    gmax = functools.reduce(jnp.maximum, gscore)
    taken = None
    in_group = []
    for g in range(N_GROUPS):
        hit = gscore[g] == gmax
        if taken is not None:
            hit = hit & ~taken
        taken = hit if taken is None else taken | hit
        in_group.append(hit)
    masked = [jnp.where(in_group[e // EXPERTS_PER_GROUP], sel[e], -jnp.inf) for e in range(N_EXPERTS)]

    def first_max(vals):
        top = functools.reduce(jnp.maximum, vals)
        seen = None
        picks = []
        for v in vals:
            hit = v == top
            if seen is not None:
                hit = hit & ~seen
            seen = hit if seen is None else seen | hit
            picks.append(hit)
        return picks

    pick1 = first_max(masked)
    rest = [jnp.where(pick1[e], -jnp.inf, masked[e]) for e in range(N_EXPERTS)]
    pick2 = first_max(rest)
    chosen = [pick1[e] | pick2[e] for e in range(N_EXPERTS)]
    gate = [jnp.where(chosen[e], score[e], 0.0) for e in range(N_EXPERTS)]
    total = functools.reduce(lambda a, b: a + b, gate)
    return jnp.concatenate([gt / total for gt in gate], axis=0)


def _outproj_kernel(x_ref, oa_ref, ob_ref, oc_ref, wa_ref, wb_ref, wc_ref, g_ref,
                    wrh_ref, wrl_ref, rb_ref, h_ref, hn_ref, comb_ref, *, tm):
    h = x_ref[...] + _dot(oa_ref[...], wa_ref[...]) + _dot(ob_ref[...], wb_ref[...]) \
        + _dot(oc_ref[...], wc_ref[...])
    h_ref[...] = h
    ms = jnp.mean(h * h, axis=-1, keepdims=True)
    hn = h * lax.rsqrt(ms + EPS) * g_ref[...]
    hi, lo = _split_bf16(hn)
    hn_ref[...] = hi
    wrh = wrh_ref[...]
    logits_t = _dot_nt(wrh, hi) + _dot_nt(wrh, lo) + _dot_nt(wrl_ref[...], hi)
    comb = _route(logits_t, rb_ref[...])
    pad = jnp.zeros((LANE - N_EXPERTS, tm), F32)
    comb_ref[...] = jnp.concatenate([comb, pad], axis=0).T


def _outproj(x, oa, ob, oc, wa, wb, wc, g, wrh, wrl, rb, *, tm):
    s = x.shape[0]
    row = lambda i: (i, 0)
    fixed = lambda i: (0, 0)
    return pl.pallas_call(
        functools.partial(_outproj_kernel, tm=tm),
        grid=(s // tm,),
        in_specs=[
            pl.BlockSpec((tm, D_MODEL), row),
            pl.BlockSpec((tm, H_DIFF * LANE), row),
            pl.BlockSpec((tm, H_SB * LANE), row),
            pl.BlockSpec((tm, H_FOX * LANE), row),
            pl.BlockSpec((H_DIFF * LANE, D_MODEL), fixed),
            pl.BlockSpec((H_SB * LANE, D_MODEL), fixed),
            pl.BlockSpec((H_FOX * LANE, D_MODEL), fixed),
            pl.BlockSpec((1, D_MODEL), fixed),
            pl.BlockSpec((N_EXPERTS, D_MODEL), fixed),
            pl.BlockSpec((N_EXPERTS, D_MODEL), fixed),
            pl.BlockSpec((N_EXPERTS, 1), fixed),
        ],
        out_specs=[
            pl.BlockSpec((tm, D_MODEL), row),
            pl.BlockSpec((tm, D_MODEL), row),
            pl.BlockSpec((tm, LANE), row),
        ],
        out_shape=[
            jax.ShapeDtypeStruct((s, D_MODEL), F32),
            jax.ShapeDtypeStruct((s, D_MODEL), BF16),
            jax.ShapeDtypeStruct((s, LANE), F32),
        ],
        compiler_params=pltpu.CompilerParams(
            dimension_semantics=("arbitrary",), vmem_limit_bytes=VMEM_LIMIT),
        name="outproj_router",
    )(x, oa, ob, oc, wa, wb, wc, g, wrh, wrl, rb)


def _moe_kernel(h_ref, hn_ref, comb_ref, wg_ref, wu_ref, wd_ref, o_ref):
    e = pl.program_id(1)

    @pl.when(e == 0)
    def _():
        o_ref[...] = h_ref[...]

    hn = hn_ref[...]
    comb = comb_ref[...]
    lane = lax.broadcasted_iota(jnp.int32, comb.shape, 1)
    weight = jnp.sum(jnp.where(lane == e, comb, 0.0), axis=-1, keepdims=True)
    gate = _dot(hn, wg_ref[...])
    up = _dot(hn, wu_ref[...])
    act = gate * jax.nn.sigmoid(gate) * up * weight
    o_ref[...] += _dot(act.astype(BF16), wd_ref[...])


def _moe(h, hn, comb, wg, wu, wd, *, tm):
    s = h.shape[0]
    row = lambda i, e: (i, 0)
    return pl.pallas_call(
        _moe_kernel,
        grid=(s // tm, N_EXPERTS),
        in_specs=[
            pl.BlockSpec((tm, D_MODEL), row),
            pl.BlockSpec((tm, D_MODEL), row),
            pl.BlockSpec((tm, LANE), row),
            pl.BlockSpec((None, D_MODEL, D_EXPERT), lambda i, e: (e, 0, 0)),
            pl.BlockSpec((None, D_MODEL, D_EXPERT), lambda i, e: (e, 0, 0)),
            pl.BlockSpec((None, D_EXPERT, D_MODEL), lambda i, e: (e, 0, 0)),
        ],
        out_specs=pl.BlockSpec((tm, D_MODEL), row),
        out_shape=jax.ShapeDtypeStruct((s, D_MODEL), F32),
        compiler_params=pltpu.CompilerParams(
            dimension_semantics=("arbitrary", "arbitrary"), vmem_limit_bytes=VMEM_LIMIT),
        name="moe_experts",
    )(h, hn, comb, wg, wu, wd)


def _proj_epilogue_tables(qnorm_diff, knorm_diff, qnorm_fox, knorm_fox):
    ones = jnp.ones((HEAD_DIM,), F32)
    zeros = jnp.zeros((HEAD_DIM,), F32)
    half = jnp.arange(HEAD_DIM) // DIFF_HALF
    g_diff = (half[:, None] == half[None, :]).astype(F32) / DIFF_HALF
    g_full = jnp.full((HEAD_DIM, HEAD_DIM), 1.0 / HEAD_DIM, F32)
    g_none = jnp.zeros((HEAD_DIM, HEAD_DIM), F32)
    diff_scale = DIFF_HALF ** -0.5
    full_scale = HEAD_DIM ** -0.5
    groups = [
        (H_DIFF, g_diff, ones, jnp.tile(qnorm_diff, 2) * diff_scale),
        (H_DIFF, g_diff, ones, jnp.tile(knorm_diff, 2)),
        (H_DIFF, g_none, zeros, ones),
        (H_SB, g_none, zeros, ones * full_scale),
        (H_SB, g_none, zeros, ones),
        (H_SB, g_none, zeros, ones),
        (H_FOX, g_full, ones, qnorm_fox * full_scale),
        (H_FOX, g_full, ones, knorm_fox),
        (H_FOX, g_none, zeros, ones),
    ]
    gmat = jnp.concatenate([jnp.broadcast_to(gm, (n, HEAD_DIM, HEAD_DIM)) for n, gm, _, _ in groups])
    flag = jnp.concatenate([jnp.tile(fl, n) for n, _, fl, _ in groups])[None, :]
    gain = jnp.concatenate([jnp.tile(gn, n) for n, _, _, gn in groups])[None, :]
    return gmat.astype(BF16), flag, gain


def _tiles(s):
    return dict(proj_tm=min(512, s), proj_tn=768, attn_t=min(512, s), sb_tc=256,
                out_tm=min(256, s), moe_tm=min(512, s))


def kernel(x, attn_norm, w_in, b_forget, qnorm_diff, knorm_diff, lam_q1, lam_k1, lam_q2, lam_k2,
           subln_diff, onorm_sb, qnorm_fox, knorm_fox, onorm_fox, w_out, ffn_norm,
           w_router, router_bias, w_gate, w_up, w_down):
    b, s, d = x.shape
    assert b == 1 and d == D_MODEL
    depth = w_in.shape[0]
    tl = _tiles(s)
    slopes = jnp.exp2(-8.0 * jnp.arange(1, H_DIFF + 1, dtype=F32) / H_DIFF)
    wr_t = w_router.T.astype(F32)
    wrh = wr_t.astype(BF16)
    wrl = (wr_t - wrh.astype(F32)).astype(BF16)
    rb = router_bias.astype(F32)[:, None]
    xs = x[0]
    for l in range(depth):
        w_main = w_in[l][:, :N_MAIN].astype(BF16)
        wfc_t = jnp.zeros((LANE, D_MODEL), F32).at[:H_FOX].set(w_in[l][:, N_MAIN:].T).astype(BF16)
        bfc = jnp.zeros((LANE, 1), F32).at[:H_FOX, 0].set(b_forget[l])
        gmat, flag, gain = _proj_epilogue_tables(qnorm_diff[l], knorm_diff[l], qnorm_fox[l], knorm_fox[l])
        proj, dcum = _proj(xs, attn_norm[l][None, :], w_main, wfc_t, bfc, gmat, flag, gain,
                           tm=tl["proj_tm"], tn=tl["proj_tn"])

        lam_init = 0.8 - 0.6 * math.exp(-0.3 * l)
        lam = (jnp.exp(jnp.sum(lam_q1[l] * lam_k1[l])) - jnp.exp(jnp.sum(lam_q2[l] * lam_k2[l]))
               + lam_init).reshape(1).astype(F32)
        oa = _diff(proj, slopes, lam, (subln_diff[l] * (1.0 - lam_init))[None, :], t=tl["attn_t"])
        ob = _sb(proj, onorm_sb[l][None, :], t=tl["attn_t"], tc=min(tl["sb_tc"], tl["attn_t"]))
        dc = dcum[:8]
        oc = _fox(proj, dc.reshape(8, s, 1), dc.reshape(8, 1, s), onorm_fox[l][None, :], t=tl["attn_t"])

        wo = w_out[l].astype(BF16)
        na, nb = H_DIFF * LANE, (H_DIFF + H_SB) * LANE
        h, hn, comb = _outproj(xs, oa, ob, oc, wo[:na], wo[na:nb], wo[nb:], ffn_norm[l][None, :],
                               wrh, wrl, rb, tm=tl["out_tm"])
        xs = _moe(h, hn, comb, w_gate[l].astype(BF16), w_up[l].astype(BF16), w_down[l].astype(BF16),
                  tm=tl["moe_tm"])
    return xs[None]
```

```python
import functools
import math

import jax
import jax.numpy as jnp
from jax import lax
from jax.experimental import pallas as pl
from jax.experimental.pallas import tpu as pltpu

F32 = jnp.float32
BF16 = jnp.bfloat16

D_MODEL = 2048
HEAD_DIM = 128
LANE = 128
BF16_ROWS = 16
N_HEADS = 16
H_SB = N_HEADS // 3
H_FOX = N_HEADS // 3
H_DIFF = N_HEADS - H_SB - H_FOX
DIFF_HALF = HEAD_DIM // 2
CHUNK = 64
N_EXPERTS = 16
N_GROUPS = 4
EXPERTS_PER_GROUP = N_EXPERTS // N_GROUPS
D_EXPERT = D_MODEL // 4
EPS = 1e-6
LOG2E = 1.4426950408889634

QA, KA = 0, H_DIFF
QB, KB = 2 * H_DIFF, 2 * H_DIFF + H_SB
QC, KC = 2 * H_DIFF + 2 * H_SB, 2 * H_DIFF + 2 * H_SB + H_FOX
N_QK = 2 * N_HEADS * HEAD_DIM
VA, VB, VC = 0, H_DIFF, H_DIFF + H_SB
N_V = N_HEADS * HEAD_DIM

VMEM_LIMIT = 48 * 1024 * 1024

_NT = (((1,), (1,)), ((), ()))


def _dot(a, b):
    return jnp.dot(a, b, preferred_element_type=F32)


def _dot_nt(a, b):
    return lax.dot_general(a, b, _NT, preferred_element_type=F32)


def _log_sigmoid(x):
    return jnp.minimum(x, 0.0) - jnp.log1p(jnp.exp(-jnp.abs(x)))


def _split_bf16(x):
    hi = x.astype(BF16)
    lo = (x - hi.astype(F32)).astype(BF16)
    return hi, lo


def _split3_bf16(x):
    hi = x.astype(BF16)
    r = x - hi.astype(F32)
    mid = r.astype(BF16)
    lo = (r - mid.astype(F32)).astype(BF16)
    return hi, mid, lo


def _proj_qk_kernel(x_ref, g_ref, w_ref, wfc_ref, bfc_ref, gmat_ref, flag_ref, gain_ref,
                    o_ref, xn_ref, dtok_ref, carry_scr, *, tm, tn):
    i = pl.program_id(0)
    j = pl.program_id(1)

    @pl.when(j == 0)
    def _():
        x = x_ref[...]
        ms = jnp.mean(x * x, axis=-1, keepdims=True)
        xn = (x * lax.rsqrt(ms + EPS) * g_ref[...]).astype(BF16)
        xn_ref[...] = xn

        @pl.when(i == 0)
        def _():
            carry_scr[...] = jnp.zeros_like(carry_scr)

        lf = _log_sigmoid(_dot_nt(wfc_ref[...], xn) + bfc_ref[...])
        hi, lo = _split_bf16(lf)
        r = lax.broadcasted_iota(jnp.int32, (tm, tm), 0)
        c = lax.broadcasted_iota(jnp.int32, (tm, tm), 1)
        tri = (r <= c).astype(BF16)
        cum = _dot(hi, tri) + _dot(lo, tri) + carry_scr[...]
        carry_scr[...] = cum[:, tm - 1:tm]
        dtok_ref[...] = (cum * LOG2E).T

    acc = _dot(xn_ref[...], w_ref[...])
    for c in range(tn // LANE):
        sl = slice(c * LANE, (c + 1) * LANE)
        y = acc[:, sl]
        ss = _dot((y * y).astype(BF16), gmat_ref[c])
        flag = flag_ref[:, sl]
        inv = flag * lax.rsqrt(ss + EPS) + (1.0 - flag)
        o_ref[:, sl] = (y * inv * gain_ref[:, sl]).astype(BF16)


def _proj_qk(x, g, w, wfc_t, bfc, gmat, flag, gain, *, tm, tn):
    s = x.shape[0]
    return pl.pallas_call(
        functools.partial(_proj_qk_kernel, tm=tm, tn=tn),
        grid=(s // tm, N_QK // tn),
        in_specs=[
            pl.BlockSpec((tm, D_MODEL), lambda i, j: (i, 0)),
            pl.BlockSpec((1, D_MODEL), lambda i, j: (0, 0)),
            pl.BlockSpec((D_MODEL, tn), lambda i, j: (0, j)),
            pl.BlockSpec((LANE, D_MODEL), lambda i, j: (0, 0)),
            pl.BlockSpec((LANE, 1), lambda i, j: (0, 0)),
            pl.BlockSpec((tn // LANE, LANE, LANE), lambda i, j: (j, 0, 0)),
            pl.BlockSpec((1, tn), lambda i, j: (0, j)),
            pl.BlockSpec((1, tn), lambda i, j: (0, j)),
        ],
        out_specs=[
            pl.BlockSpec((tm, tn), lambda i, j: (i, j)),
            pl.BlockSpec((tm, D_MODEL), lambda i, j: (i, 0)),
            pl.BlockSpec((tm, LANE), lambda i, j: (i, 0)),
        ],
        out_shape=[
            jax.ShapeDtypeStruct((s, N_QK), BF16),
            jax.ShapeDtypeStruct((s, D_MODEL), BF16),
            jax.ShapeDtypeStruct((s, LANE), F32),
        ],
        scratch_shapes=[pltpu.VMEM((LANE, 1), F32)],
        compiler_params=pltpu.CompilerParams(
            dimension_semantics=("arbitrary", "arbitrary"), vmem_limit_bytes=VMEM_LIMIT),
        name="proj_qk",
    )(x, g, w, wfc_t, bfc, gmat, flag, gain)


def _proj_v_kernel(xn_ref, wt_ref, o_ref):
    o_ref[...] = _dot_nt(wt_ref[...], xn_ref[...]).astype(BF16)


def _proj_v(xn, wv_t, *, tm, tv):
    s = xn.shape[0]
    return pl.pallas_call(
        _proj_v_kernel,
        grid=(s // tm, N_V // tv),
        in_specs=[
            pl.BlockSpec((tm, D_MODEL), lambda i, j: (i, 0)),
            pl.BlockSpec((tv, D_MODEL), lambda i, j: (j, 0)),
        ],
        out_specs=pl.BlockSpec((tv, tm), lambda i, j: (j, i)),
        out_shape=jax.ShapeDtypeStruct((N_V, s), BF16),
        compiler_params=pltpu.CompilerParams(
            dimension_semantics=("arbitrary", "arbitrary"), vmem_limit_bytes=VMEM_LIMIT),
        name="proj_v",
    )(xn, wv_t)


def _bias_lanes(split, ones_value, *, split_first):
    hi, mid, lo = split
    lane = lax.broadcasted_iota(jnp.int32, hi.shape, 1)
    s0, c0 = (0, 3) if split_first else (3, 0)
    const = jnp.where((lane >= c0) & (lane < c0 + 3), ones_value, 0.0).astype(BF16)
    return jnp.where(lane == s0, hi, jnp.where(lane == s0 + 1, mid, jnp.where(lane == s0 + 2, lo, const)))


def _kq_iotas(t):
    return (lax.broadcasted_iota(jnp.int32, (t, t), 0),
            lax.broadcasted_iota(jnp.int32, (t, t), 1))


def _fold_rows(x, op):
    while x.shape[0] > 8:
        half = x.shape[0] // 2
        x = op(x[:half], x[half:])
    return x


def _softmax_step(s, vt, m_scr, acc_scr):
    m_old = m_scr[...]
    m_new = jnp.maximum(m_old, jnp.max(_fold_rows(s, jnp.maximum), axis=0, keepdims=True))
    alpha = jnp.exp2(m_old - m_new)
    p = jnp.exp2(s - m_new).astype(BF16)
    vt_aug = jnp.concatenate([vt, jnp.ones((BF16_ROWS, vt.shape[1]), BF16)], axis=0)
    acc_scr[...] = alpha * acc_scr[...] + _dot(vt_aug, p)
    m_scr[...] = m_new


def _finish_head(o_t, gcol_ref, o_ref):
    ms = jnp.mean(o_t * o_t, axis=0, keepdims=True)
    o_ref[...] = (o_t * lax.rsqrt(ms + EPS) * gcol_ref[...]).T.astype(BF16)


def _pipelined_blocks(qi, issue_scores, consume):
    issue_scores(0, 0)

    def body(p, carry):
        issue_scores(2 * p + 1, 1)
        consume(2 * p, 0, False)
        issue_scores(2 * p + 2, 0)
        consume(2 * p + 1, 1, False)
        return carry

    lax.fori_loop(0, qi // 2, body, 0)

    @pl.when(qi % 2 == 0)
    def _():
        consume(qi, 0, True)

    @pl.when(qi % 2 == 1)
    def _():
        issue_scores(qi, 1)
        consume(qi - 1, 0, False)
        consume(qi, 1, True)


def _fox_kernel(q_ref, k_ref, vt_ref, d_ref, g_ref, o_ref, kaug_scr, s_scr, m_scr, acc_scr, *, t, nblk):
    h = pl.program_id(0)
    qi = pl.program_id(1)
    pick = (lax.broadcasted_iota(jnp.int32, (LANE, LANE), 0) == h).astype(BF16)

    def head_bias(rows):
        return tuple(_dot(p, pick).astype(BF16) for p in _split3_bf16(d_ref[rows, :]))

    @pl.when(qi == 0)
    def _():
        def build(b, carry):
            rows = pl.ds(pl.multiple_of(b * t, t), t)
            kaug_scr[rows, :LANE] = k_ref[rows, :]
            kaug_scr[rows, LANE:] = _bias_lanes(head_bias(rows), 1.0, split_first=True)
            return carry
        lax.fori_loop(0, nblk, build, 0)

    qrows = pl.ds(pl.multiple_of(qi * t, t), t)
    q_aug = jnp.concatenate([q_ref[...], _bias_lanes(head_bias(qrows), -1.0, split_first=False)], axis=1)
    m_scr[...] = jnp.full_like(m_scr, -jnp.inf)
    acc_scr[...] = jnp.zeros_like(acc_scr)

    def issue_scores(j, slot):
        s_scr[slot] = _dot_nt(kaug_scr[pl.ds(pl.multiple_of(j * t, t), t), :], q_aug)

    def consume(j, slot, masked):
        s = s_scr[slot]
        if masked:
            kpos, qpos = _kq_iotas(t)
            s = jnp.where(kpos <= qpos, s, -jnp.inf)
        _softmax_step(s, vt_ref[:, pl.ds(pl.multiple_of(j * t, t), t)], m_scr, acc_scr)

    _pipelined_blocks(qi, issue_scores, consume)
    acc = acc_scr[...]
    _finish_head(acc[:HEAD_DIM] / acc[HEAD_DIM:HEAD_DIM + 1], g_ref, o_ref)


def _diff_kernel(slope_ref, lam_ref, q_ref, k_ref, vt_ref, g_ref, o_ref,
                 kaug_scr, s_scr, m1_scr, a1_scr, m2_scr, a2_scr, *, t, nblk):
    h = pl.program_id(0)
    qi = pl.program_id(1)
    slope = slope_ref[h]
    lam = lam_ref[0]

    def pos_split(block, sign):
        pos = (block * t + lax.broadcasted_iota(jnp.int32, (t, LANE), 0)).astype(F32)
        return _split3_bf16(sign * slope * pos)

    @pl.when(qi == 0)
    def _():
        def build(b, carry):
            rows = pl.ds(pl.multiple_of(b * t, t), t)
            kaug_scr[rows, :LANE] = k_ref[rows, :]
            kaug_scr[rows, LANE:] = _bias_lanes(pos_split(b, 1.0), 1.0, split_first=True)
            return carry
        lax.fori_loop(0, nblk, build, 0)

    q = q_ref[...]
    lane = lax.broadcasted_iota(jnp.int32, q.shape, 1)
    zero = jnp.zeros_like(q)
    q_bias = _bias_lanes(pos_split(qi, -1.0), 1.0, split_first=False)
    q1_aug = jnp.concatenate([jnp.where(lane < DIFF_HALF, q, zero), q_bias], axis=1)
    q2_aug = jnp.concatenate([jnp.where(lane >= DIFF_HALF, q, zero), q_bias], axis=1)
    for m_scr, a_scr in ((m1_scr, a1_scr), (m2_scr, a2_scr)):
        m_scr[...] = jnp.full_like(m_scr, -jnp.inf)
        a_scr[...] = jnp.zeros_like(a_scr)

    def issue_scores(j, slot):
        kb = kaug_scr[pl.ds(pl.multiple_of(j * t, t), t), :]
        s_scr[slot, 0] = _dot_nt(kb, q1_aug)
        s_scr[slot, 1] = _dot_nt(kb, q2_aug)

    def consume(j, slot, masked):
        vt = vt_ref[:, pl.ds(pl.multiple_of(j * t, t), t)]
        for idx, (m_scr, a_scr) in enumerate(((m1_scr, a1_scr), (m2_scr, a2_scr))):
            s = s_scr[slot, idx]
            if masked:
                kpos, qpos = _kq_iotas(t)
                ahead = (kpos - qpos).astype(F32)
                s = s - jnp.where(ahead > 0.0, (2.0 * slope) * ahead, 0.0)
                s = jnp.where((kpos // CHUNK) <= (qpos // CHUNK), s, -jnp.inf)
            _softmax_step(s, vt, m_scr, a_scr)

    _pipelined_blocks(qi, issue_scores, consume)
    a1 = a1_scr[...]
    a2 = a2_scr[...]
    o_t = (a1[:HEAD_DIM] / a1[HEAD_DIM:HEAD_DIM + 1]
           - lam * (a2[:HEAD_DIM] / a2[HEAD_DIM:HEAD_DIM + 1]))
    _finish_head(o_t, g_ref, o_ref)


def _sb_kernel(q_ref, k_ref, vt_ref, g_ref, o_ref, s_scr, run_scr, acc_scr, *, t, tc):
    qi = pl.program_id(1)
    q = q_ref[...]
    run_scr[...] = jnp.zeros_like(run_scr)
    acc_scr[...] = jnp.zeros_like(acc_scr)
    ur = lax.broadcasted_iota(jnp.int32, (tc, tc), 0)
    uc = lax.broadcasted_iota(jnp.int32, (tc, tc), 1)
    later = (uc > ur).astype(BF16)
    nchunk = t // tc

    def issue_scores(j, slot):
        s_scr[slot] = _dot_nt(k_ref[pl.ds(pl.multiple_of(j * t, t), t), :], q)

    def consume(j, slot, masked):
        z = s_scr[slot]
        log_beta = jnp.minimum(z, 0.0) - jnp.log2(1.0 + jnp.exp2(-jnp.abs(z)))
        log_keep = log_beta - z
        if masked:
            kpos, qpos = _kq_iotas(t)
            strict = kpos < qpos
            log_keep = jnp.where(strict, log_keep, 0.0)
        run = run_scr[...]
        parts = []
        for ci in range(nchunk - 1, -1, -1):
            sl = slice(ci * tc, (ci + 1) * tc)
            lk = log_keep[sl]
            hi, lo = _split_bf16(lk)
            between = _dot(later, hi) + _dot(later, lo) + run
            parts.append(jnp.exp2(log_beta[sl] + between))
            run = run + jnp.sum(_fold_rows(lk, jnp.add), axis=0, keepdims=True)
        a = jnp.concatenate(parts[::-1], axis=0)
        if masked:
            a = jnp.where(strict, a, 0.0)
        acc_scr[...] += _dot(vt_ref[:, pl.ds(pl.multiple_of(j * t, t), t)], a.astype(BF16))
        run_scr[...] = run

    issue_scores(qi, 0)
    issue_scores(jnp.maximum(qi - 1, 0), 1)
    consume(qi, 0, True)

    def rest(p, carry):
        issue_scores(jnp.maximum(qi - 2 * p - 2, 0), 0)
        consume(qi - 2 * p - 1, 1, False)
        issue_scores(jnp.maximum(qi - 2 * p - 3, 0), 1)
        consume(qi - 2 * p - 2, 0, False)
        return carry

    lax.fori_loop(0, qi // 2, rest, 0)

    @pl.when(qi % 2 == 1)
    def _():
        consume(0, 1, False)
    _finish_head(acc_scr[...], g_ref, o_ref)


def _attn_specs(s, t, qc, kc, vc):
    return [
        pl.BlockSpec((t, LANE), lambda h, i: (i, qc + h)),
        pl.BlockSpec((s, LANE), lambda h, i: (0, kc + h)),
        pl.BlockSpec((LANE, s), lambda h, i: (vc + h, 0)),
    ]


_ATTN_PARAMS = pltpu.CompilerParams(
    dimension_semantics=("arbitrary", "arbitrary"), vmem_limit_bytes=VMEM_LIMIT)
_GCOL_SPEC = pl.BlockSpec((LANE, 1), lambda h, i: (0, 0))


def _fox(qk, vt, dtok, gcol, *, t):
    s = qk.shape[0]
    return pl.pallas_call(
        functools.partial(_fox_kernel, t=t, nblk=s // t),
        grid=(H_FOX, s // t),
        in_specs=_attn_specs(s, t, QC, KC, VC) + [
            pl.BlockSpec((s, LANE), lambda h, i: (0, 0)),
            _GCOL_SPEC,
        ],
        out_specs=pl.BlockSpec((t, LANE), lambda h, i: (i, h)),
        out_shape=jax.ShapeDtypeStruct((s, H_FOX * LANE), BF16),
        scratch_shapes=[
            pltpu.VMEM((s, 2 * LANE), BF16),
            pltpu.VMEM((2, t, t), F32),
            pltpu.VMEM((1, t), F32),
            pltpu.VMEM((HEAD_DIM + BF16_ROWS, t), F32),
        ],
        compiler_params=_ATTN_PARAMS,
        name="fox_attn",
    )(qk, qk, vt, dtok, gcol)


def _diff(qk, vt, slopes, lam, gcol, *, t):
    s = qk.shape[0]
    row = pltpu.VMEM((1, t), F32)
    acc = pltpu.VMEM((HEAD_DIM + BF16_ROWS, t), F32)
    smem = pl.BlockSpec(memory_space=pltpu.SMEM)
    return pl.pallas_call(
        functools.partial(_diff_kernel, t=t, nblk=s // t),
        grid=(H_DIFF, s // t),
        in_specs=[smem, smem] + _attn_specs(s, t, QA, KA, VA) + [_GCOL_SPEC],
        out_specs=pl.BlockSpec((t, LANE), lambda h, i: (i, h)),
        out_shape=jax.ShapeDtypeStruct((s, H_DIFF * LANE), BF16),
        scratch_shapes=[
            pltpu.VMEM((s, 2 * LANE), BF16),
            pltpu.VMEM((2, 2, t, t), F32),
            row, acc, row, acc,
        ],
        compiler_params=_ATTN_PARAMS,
        name="diff_attn",
    )(slopes, lam, qk, qk, vt, gcol)


def _sb(qk, vt, gcol, *, t, tc):
    s = qk.shape[0]
    return pl.pallas_call(
        functools.partial(_sb_kernel, t=t, tc=tc),
        grid=(H_SB, s // t),
        in_specs=_attn_specs(s, t, QB, KB, VB) + [_GCOL_SPEC],
        out_specs=pl.BlockSpec((t, LANE), lambda h, i: (i, h)),
        out_shape=jax.ShapeDtypeStruct((s, H_SB * LANE), BF16),
        scratch_shapes=[
            pltpu.VMEM((2, t, t), F32),
            pltpu.VMEM((1, t), F32),
            pltpu.VMEM((HEAD_DIM, t), F32),
        ],
        compiler_params=_ATTN_PARAMS,
        name="sb_attn",
    )(qk, qk, vt, gcol)


def _route(logits_t, bias):
    score = [jax.nn.sigmoid(logits_t[e:e + 1, :]) for e in range(N_EXPERTS)]
    sel = [score[e] + bias[e:e + 1, :] for e in range(N_EXPERTS)]
    gscore = []
    for g in range(N_GROUPS):
        mem = sel[g * EXPERTS_PER_GROUP:(g + 1) * EXPERTS_PER_GROUP]
        best = None
        for a in range(EXPERTS_PER_GROUP):
            for b in range(a + 1, EXPERTS_PER_GROUP):
                pair = mem[a] + mem[b]
                best = pair if best is None else jnp.maximum(best, pair)
        gscore.append(best)
    gmax = functools.reduce(jnp.maximum, gscore)
    taken = None
    in_group = []
    for g in range(N_GROUPS):
        hit = gscore[g] == gmax
        if taken is not None:
            hit = hit & ~taken
        taken = hit if taken is None else taken | hit
        in_group.append(hit)
    masked = [jnp.where(in_group[e // EXPERTS_PER_GROUP], sel[e], -jnp.inf) for e in range(N_EXPERTS)]

    def first_max(vals):
        top = functools.reduce(jnp.maximum, vals)
        seen = None
        picks = []
        for v in vals:
            hit = v == top
            if seen is not None:
                hit = hit & ~seen
            seen = hit if seen is None else seen | hit
            picks.append(hit)
        return picks

    pick1 = first_max(masked)
    rest = [jnp.where(pick1[e], -jnp.inf, masked[e]) for e in range(N_EXPERTS)]
    pick2 = first_max(rest)
    chosen = [pick1[e] | pick2[e] for e in range(N_EXPERTS)]
    gate = [jnp.where(chosen[e], score[e], 0.0) for e in range(N_EXPERTS)]
    total = functools.reduce(lambda a, b: a + b, gate)
    return jnp.concatenate([gt / total for gt in gate], axis=0)


def _outproj_kernel(x_ref, oa_ref, ob_ref, oc_ref, wa_ref, wb_ref, wc_ref, g_ref,
                    wrh_ref, wrl_ref, rb_ref, h_ref, hn_ref, comb_ref, *, tm):
    h = x_ref[...] + _dot(oa_ref[...], wa_ref[...]) + _dot(ob_ref[...], wb_ref[...]) \
        + _dot(oc_ref[...], wc_ref[...])
    h_ref[...] = h
    ms = jnp.mean(h * h, axis=-1, keepdims=True)
    hn = h * lax.rsqrt(ms + EPS) * g_ref[...]
    hi, lo = _split_bf16(hn)
    hn_ref[...] = hi
    wrh = wrh_ref[...]
    logits_t = _dot_nt(wrh, hi) + _dot_nt(wrh, lo) + _dot_nt(wrl_ref[...], hi)
    comb = _route(logits_t, rb_ref[...])
    pad = jnp.zeros((LANE - N_EXPERTS, tm), F32)
    comb_ref[...] = jnp.concatenate([comb, pad], axis=0).T


def _outproj(x, oa, ob, oc, wa, wb, wc, g, wrh, wrl, rb, *, tm):
    s = x.shape[0]
    row = lambda i: (i, 0)
    fixed = lambda i: (0, 0)
    return pl.pallas_call(
        functools.partial(_outproj_kernel, tm=tm),
        grid=(s // tm,),
        in_specs=[
            pl.BlockSpec((tm, D_MODEL), row),
            pl.BlockSpec((tm, H_DIFF * LANE), row),
            pl.BlockSpec((tm, H_SB * LANE), row),
            pl.BlockSpec((tm, H_FOX * LANE), row),
            pl.BlockSpec((H_DIFF * LANE, D_MODEL), fixed),
            pl.BlockSpec((H_SB * LANE, D_MODEL), fixed),
            pl.BlockSpec((H_FOX * LANE, D_MODEL), fixed),
            pl.BlockSpec((1, D_MODEL), fixed),
            pl.BlockSpec((N_EXPERTS, D_MODEL), fixed),
            pl.BlockSpec((N_EXPERTS, D_MODEL), fixed),
            pl.BlockSpec((N_EXPERTS, 1), fixed),
        ],
        out_specs=[
            pl.BlockSpec((tm, D_MODEL), row),
            pl.BlockSpec((tm, D_MODEL), row),
            pl.BlockSpec((tm, LANE), row),
        ],
        out_shape=[
            jax.ShapeDtypeStruct((s, D_MODEL), F32),
            jax.ShapeDtypeStruct((s, D_MODEL), BF16),
            jax.ShapeDtypeStruct((s, LANE), F32),
        ],
        compiler_params=pltpu.CompilerParams(
            dimension_semantics=("arbitrary",), vmem_limit_bytes=VMEM_LIMIT),
        name="outproj_router",
    )(x, oa, ob, oc, wa, wb, wc, g, wrh, wrl, rb)


def _moe_kernel(h_ref, hn_ref, comb_ref, wg_ref, wu_ref, wd_ref, o_ref):
    e = pl.program_id(1)

    @pl.when(e == 0)
    def _():
        o_ref[...] = h_ref[...]

    hn = hn_ref[...]
    comb = comb_ref[...]
    lane = lax.broadcasted_iota(jnp.int32, comb.shape, 1)
    weight = jnp.sum(jnp.where(lane == e, comb, 0.0), axis=-1, keepdims=True)
    gate = _dot(hn, wg_ref[...])
    up = _dot(hn, wu_ref[...])
    act = gate * jax.nn.sigmoid(gate) * up * weight
    o_ref[...] += _dot(act.astype(BF16), wd_ref[...])


def _moe(h, hn, comb, wg, wu, wd, *, tm):
    s = h.shape[0]
    row = lambda i, e: (i, 0)
    return pl.pallas_call(
        _moe_kernel,
        grid=(s // tm, N_EXPERTS),
        in_specs=[
            pl.BlockSpec((tm, D_MODEL), row),
            pl.BlockSpec((tm, D_MODEL), row),
            pl.BlockSpec((tm, LANE), row),
            pl.BlockSpec((None, D_MODEL, D_EXPERT), lambda i, e: (e, 0, 0)),
            pl.BlockSpec((None, D_MODEL, D_EXPERT), lambda i, e: (e, 0, 0)),
            pl.BlockSpec((None, D_EXPERT, D_MODEL), lambda i, e: (e, 0, 0)),
        ],
        out_specs=pl.BlockSpec((tm, D_MODEL), row),
        out_shape=jax.ShapeDtypeStruct((s, D_MODEL), F32),
        compiler_params=pltpu.CompilerParams(
            dimension_semantics=("arbitrary", "arbitrary"), vmem_limit_bytes=VMEM_LIMIT),
        name="moe_experts",
    )(h, hn, comb, wg, wu, wd)


def _proj_epilogue_tables(qnorm_diff, knorm_diff, qnorm_fox, knorm_fox):
    ones = jnp.ones((HEAD_DIM,), F32)
    zeros = jnp.zeros((HEAD_DIM,), F32)
    half = jnp.arange(HEAD_DIM) // DIFF_HALF
    g_diff = (half[:, None] == half[None, :]).astype(F32) / DIFF_HALF
    g_full = jnp.full((HEAD_DIM, HEAD_DIM), 1.0 / HEAD_DIM, F32)
    g_none = jnp.zeros((HEAD_DIM, HEAD_DIM), F32)
    diff_scale = DIFF_HALF ** -0.5 * LOG2E
    full_scale = HEAD_DIM ** -0.5 * LOG2E
    groups = [
        (H_DIFF, g_diff, ones, jnp.tile(qnorm_diff, 2) * diff_scale),
        (H_DIFF, g_diff, ones, jnp.tile(knorm_diff, 2)),
        (H_SB, g_none, zeros, ones * full_scale),
        (H_SB, g_none, zeros, ones),
        (H_FOX, g_full, ones, qnorm_fox * full_scale),
        (H_FOX, g_full, ones, knorm_fox),
    ]
    gmat = jnp.concatenate([jnp.broadcast_to(gm, (n, HEAD_DIM, HEAD_DIM)) for n, gm, _, _ in groups])
    flag = jnp.concatenate([jnp.tile(fl, n) for n, _, fl, _ in groups])[None, :]
    gain = jnp.concatenate([jnp.tile(gn, n) for n, _, _, gn in groups])[None, :]
    return gmat.astype(BF16), flag, gain


def _split_in_weights(w):
    wa, wb, wc = H_DIFF * HEAD_DIM, H_SB * HEAD_DIM, H_FOX * HEAD_DIM
    pieces = []
    off = 0
    for width in (wa, wa, wa, wb, wb, wb, wc, wc, wc):
        pieces.append(w[:, off:off + width])
        off += width
    qa, ka, va, qb, kb, vb, qc, kc, vc = pieces
    return (jnp.concatenate([qa, ka, qb, kb, qc, kc], axis=1),
            jnp.concatenate([va, vb, vc], axis=1), w[:, off:])


def _tiles(s):
    return dict(proj_tm=min(512, s), proj_tn=512, attn_t=min(512, s), sb_tc=256,
                out_tm=min(256, s), moe_tm=min(512, s))


def kernel(x, attn_norm, w_in, b_forget, qnorm_diff, knorm_diff, lam_q1, lam_k1, lam_q2, lam_k2,
           subln_diff, onorm_sb, qnorm_fox, knorm_fox, onorm_fox, w_out, ffn_norm,
           w_router, router_bias, w_gate, w_up, w_down):
    b, s, d = x.shape
    assert b == 1 and d == D_MODEL
    depth = w_in.shape[0]
    tl = _tiles(s)
    t = tl["attn_t"]
    slopes = jnp.exp2(-8.0 * jnp.arange(1, H_DIFF + 1, dtype=F32) / H_DIFF) * LOG2E
    wr_t = w_router.T.astype(F32)
    wrh = wr_t.astype(BF16)
    wrl = (wr_t - wrh.astype(F32)).astype(BF16)
    rb = router_bias.astype(F32)[:, None]
    xs = x[0]
    for l in range(depth):
        w_qk, w_v, w_fc = _split_in_weights(w_in[l])
        wfc_t = jnp.zeros((LANE, D_MODEL), F32).at[:H_FOX].set(w_fc.T).astype(BF16)
        bfc = jnp.zeros((LANE, 1), F32).at[:H_FOX, 0].set(b_forget[l])
        gmat, flag, gain = _proj_epilogue_tables(qnorm_diff[l], knorm_diff[l], qnorm_fox[l], knorm_fox[l])
        qk, xn, dtok = _proj_qk(xs, attn_norm[l][None, :], w_qk.astype(BF16), wfc_t, bfc, gmat, flag, gain,
                                tm=tl["proj_tm"], tn=tl["proj_tn"])
        vt = _proj_v(xn, w_v.T.astype(BF16), tm=tl["proj_tm"], tv=512)

        lam_init = 0.8 - 0.6 * math.exp(-0.3 * l)
        lam = (jnp.exp(jnp.sum(lam_q1[l] * lam_k1[l])) - jnp.exp(jnp.sum(lam_q2[l] * lam_k2[l]))
               + lam_init).reshape(1).astype(F32)
        oa = _diff(qk, vt, slopes, lam, (subln_diff[l] * (1.0 - lam_init))[:, None], t=t)
        ob = _sb(qk, vt, onorm_sb[l][:, None], t=t, tc=min(tl["sb_tc"], t))
        oc = _fox(qk, vt, dtok, onorm_fox[l][:, None], t=t)

        wo = w_out[l].astype(BF16)
        na, nb = H_DIFF * LANE, (H_DIFF + H_SB) * LANE
        h, hn, comb = _outproj(xs, oa, ob, oc, wo[:na], wo[na:nb], wo[nb:], ffn_norm[l][None, :],
                               wrh, wrl, rb, tm=tl["out_tm"])
        xs = _moe(h, hn, comb, w_gate[l].astype(BF16), w_up[l].astype(BF16), w_down[l].astype(BF16),
                  tm=tl["moe_tm"])
    return xs[None]
```

```python
import functools
import math

import jax
import jax.numpy as jnp
from jax import lax
from jax.experimental import pallas as pl
from jax.experimental.pallas import tpu as pltpu

F32 = jnp.float32
BF16 = jnp.bfloat16

D_MODEL = 2048
HEAD_DIM = 128
LANE = 128
BF16_ROWS = 16
N_HEADS = 16
H_SB = N_HEADS // 3
H_FOX = N_HEADS // 3
H_DIFF = N_HEADS - H_SB - H_FOX
DIFF_HALF = HEAD_DIM // 2
CHUNK = 64
N_EXPERTS = 16
N_GROUPS = 4
EXPERTS_PER_GROUP = N_EXPERTS // N_GROUPS
D_EXPERT = D_MODEL // 4
EPS = 1e-6
LOG2E = 1.4426950408889634

QA, KA = 0, H_DIFF
QB, KB = 2 * H_DIFF, 2 * H_DIFF + H_SB
QC, KC = 2 * H_DIFF + 2 * H_SB, 2 * H_DIFF + 2 * H_SB + H_FOX
N_QK = 2 * N_HEADS * HEAD_DIM
VA, VB, VC = 0, H_DIFF, H_DIFF + H_SB
N_V = N_HEADS * HEAD_DIM

VMEM_LIMIT = 48 * 1024 * 1024

_NT = (((1,), (1,)), ((), ()))


def _dot(a, b):
    return jnp.dot(a, b, preferred_element_type=F32)


def _dot_nt(a, b):
    return lax.dot_general(a, b, _NT, preferred_element_type=F32)


def _log_sigmoid(x):
    return jnp.minimum(x, 0.0) - jnp.log1p(jnp.exp(-jnp.abs(x)))


def _split_bf16(x):
    hi = x.astype(BF16)
    lo = (x - hi.astype(F32)).astype(BF16)
    return hi, lo


def _split3_bf16(x):
    hi = x.astype(BF16)
    r = x - hi.astype(F32)
    mid = r.astype(BF16)
    lo = (r - mid.astype(F32)).astype(BF16)
    return hi, mid, lo


def _proj_qk_kernel(x_ref, g_ref, w_ref, wfc_ref, bfc_ref, gmat_ref, eps_ref, gain_ref,
                    o_ref, xn_ref, dtok_ref, carry_scr, *, tm, tn):
    i = pl.program_id(0)
    j = pl.program_id(1)

    @pl.when(j == 0)
    def _():
        x = x_ref[...]
        ms = jnp.mean(x * x, axis=-1, keepdims=True)
        xn = (x * lax.rsqrt(ms + EPS) * g_ref[...]).astype(BF16)
        xn_ref[...] = xn

        @pl.when(i == 0)
        def _():
            carry_scr[...] = jnp.zeros_like(carry_scr)

        lf = _log_sigmoid(_dot_nt(wfc_ref[...], xn) + bfc_ref[...])
        hi, lo = _split_bf16(lf)
        r = lax.broadcasted_iota(jnp.int32, (tm, tm), 0)
        c = lax.broadcasted_iota(jnp.int32, (tm, tm), 1)
        tri = (r <= c).astype(BF16)
        cum = _dot(hi, tri) + _dot(lo, tri) + carry_scr[...]
        carry_scr[...] = cum[:, tm - 1:tm]
        dtok_ref[...] = (cum * LOG2E).T

    acc = _dot(xn_ref[...], w_ref[...])
    for c in range(tn // LANE):
        sl = slice(c * LANE, (c + 1) * LANE)
        y = acc[:, sl]
        ss = _dot((y * y).astype(BF16), gmat_ref[c])
        o_ref[:, sl] = (y * lax.rsqrt(ss + eps_ref[:, sl]) * gain_ref[:, sl]).astype(BF16)


def _proj_qk(x, g, w, wfc_t, bfc, gmat, eps, gain, *, tm, tn):
    s = x.shape[0]
    return pl.pallas_call(
        functools.partial(_proj_qk_kernel, tm=tm, tn=tn),
        grid=(s // tm, N_QK // tn),
        in_specs=[
            pl.BlockSpec((tm, D_MODEL), lambda i, j: (i, 0)),
            pl.BlockSpec((1, D_MODEL), lambda i, j: (0, 0)),
            pl.BlockSpec((D_MODEL, tn), lambda i, j: (0, j)),
            pl.BlockSpec((LANE, D_MODEL), lambda i, j: (0, 0)),
            pl.BlockSpec((LANE, 1), lambda i, j: (0, 0)),
            pl.BlockSpec((tn // LANE, LANE, LANE), lambda i, j: (j, 0, 0)),
            pl.BlockSpec((1, tn), lambda i, j: (0, j)),
            pl.BlockSpec((1, tn), lambda i, j: (0, j)),
        ],
        out_specs=[
            pl.BlockSpec((tm, tn), lambda i, j: (i, j)),
            pl.BlockSpec((tm, D_MODEL), lambda i, j: (i, 0)),
            pl.BlockSpec((tm, LANE), lambda i, j: (i, 0)),
        ],
        out_shape=[
            jax.ShapeDtypeStruct((s, N_QK), BF16),
            jax.ShapeDtypeStruct((s, D_MODEL), BF16),
            jax.ShapeDtypeStruct((s, LANE), F32),
        ],
        scratch_shapes=[pltpu.VMEM((LANE, 1), F32)],
        compiler_params=pltpu.CompilerParams(
            dimension_semantics=("arbitrary", "arbitrary"), vmem_limit_bytes=VMEM_LIMIT),
        name="proj_qk",
    )(x, g, w, wfc_t, bfc, gmat, eps, gain)


def _proj_v_kernel(xn_ref, wt_ref, o_ref):
    o_ref[...] = _dot_nt(wt_ref[...], xn_ref[...]).astype(BF16)


def _proj_v(xn, wv_t, *, tm, tv):
    s = xn.shape[0]
    return pl.pallas_call(
        _proj_v_kernel,
        grid=(s // tm, N_V // tv),
        in_specs=[
            pl.BlockSpec((tm, D_MODEL), lambda i, j: (i, 0)),
            pl.BlockSpec((tv, D_MODEL), lambda i, j: (j, 0)),
        ],
        out_specs=pl.BlockSpec((tv, tm), lambda i, j: (j, i)),
        out_shape=jax.ShapeDtypeStruct((N_V, s), BF16),
        compiler_params=pltpu.CompilerParams(
            dimension_semantics=("arbitrary", "arbitrary"), vmem_limit_bytes=VMEM_LIMIT),
        name="proj_v",
    )(xn, wv_t)


def _bias_lanes(split, ones_value, *, split_first):
    hi, mid, lo = split
    lane = lax.broadcasted_iota(jnp.int32, hi.shape, 1)
    s0, c0 = (0, 3) if split_first else (3, 0)
    const = jnp.where((lane >= c0) & (lane < c0 + 3), ones_value, 0.0).astype(BF16)
    return jnp.where(lane == s0, hi, jnp.where(lane == s0 + 1, mid, jnp.where(lane == s0 + 2, lo, const)))


def _kq_iotas(t):
    return (lax.broadcasted_iota(jnp.int32, (t, t), 0),
            lax.broadcasted_iota(jnp.int32, (t, t), 1))


def _fold_rows(x, op):
    while x.shape[0] > 8:
        half = x.shape[0] // 2
        x = op(x[:half], x[half:])
    return x


def _softmax_step(s, vt, m_scr, acc_scr):
    m_old = m_scr[...]
    m_new = jnp.maximum(m_old, jnp.max(_fold_rows(s, jnp.maximum), axis=0, keepdims=True))
    alpha = jnp.exp2(m_old - m_new)
    p = jnp.exp2(s - m_new).astype(BF16)
    vt_aug = jnp.concatenate([vt, jnp.ones((BF16_ROWS, vt.shape[1]), BF16)], axis=0)
    acc_scr[...] = alpha * acc_scr[...] + _dot(vt_aug, p)
    m_scr[...] = m_new


def _finish_head(o_t, gcol_ref, o_ref):
    ms = jnp.mean(o_t * o_t, axis=0, keepdims=True)
    o_ref[...] = (o_t * lax.rsqrt(ms + EPS) * gcol_ref[...]).T.astype(BF16)


def _pipelined_blocks(qi, issue_scores, consume):
    issue_scores(0, 0)

    def body(p, carry):
        issue_scores(2 * p + 1, 1)
        consume(2 * p, 0, False)
        issue_scores(2 * p + 2, 0)
        consume(2 * p + 1, 1, False)
        return carry

    lax.fori_loop(0, qi // 2, body, 0)

    @pl.when(qi % 2 == 0)
    def _():
        consume(qi, 0, True)

    @pl.when(qi % 2 == 1)
    def _():
        issue_scores(qi, 1)
        consume(qi - 1, 0, False)
        consume(qi, 1, True)


def _fox_kernel(q_ref, k_ref, vt_ref, d_ref, g_ref, o_ref, kaug_scr, s_scr, m_scr, acc_scr, *, t, nblk):
    h = pl.program_id(0)
    qi = pl.program_id(1)
    pick = (lax.broadcasted_iota(jnp.int32, (LANE, LANE), 0) == h).astype(BF16)

    def head_bias(rows):
        return tuple(_dot(p, pick).astype(BF16) for p in _split3_bf16(d_ref[rows, :]))

    @pl.when(qi == 0)
    def _():
        def build(b, carry):
            rows = pl.ds(pl.multiple_of(b * t, t), t)
            kaug_scr[rows, :LANE] = k_ref[rows, :]
            kaug_scr[rows, LANE:] = _bias_lanes(head_bias(rows), 1.0, split_first=True)
            return carry
        lax.fori_loop(0, nblk, build, 0)

    qrows = pl.ds(pl.multiple_of(qi * t, t), t)
    q_aug = jnp.concatenate([q_ref[...], _bias_lanes(head_bias(qrows), -1.0, split_first=False)], axis=1)
    m_scr[...] = jnp.full_like(m_scr, -jnp.inf)
    acc_scr[...] = jnp.zeros_like(acc_scr)

    def issue_scores(j, slot):
        s_scr[slot] = _dot_nt(kaug_scr[pl.ds(pl.multiple_of(j * t, t), t), :], q_aug)

    def consume(j, slot, masked):
        s = s_scr[slot]
        if masked:
            kpos, qpos = _kq_iotas(t)
            s = jnp.where(kpos <= qpos, s, -jnp.inf)
        _softmax_step(s, vt_ref[:, pl.ds(pl.multiple_of(j * t, t), t)], m_scr, acc_scr)

    _pipelined_blocks(qi, issue_scores, consume)
    acc = acc_scr[...]
    _finish_head(acc[:HEAD_DIM] / acc[HEAD_DIM:HEAD_DIM + 1], g_ref, o_ref)


def _diff_kernel(slope_ref, lam_ref, q_ref, k_ref, vt_ref, g_ref, o_ref,
                 kaug_scr, s_scr, m1_scr, a1_scr, m2_scr, a2_scr, *, t, nblk):
    h = pl.program_id(0)
    qi = pl.program_id(1)
    slope = slope_ref[h]
    lam = lam_ref[0]

    def pos_split(block, sign):
        pos = (block * t + lax.broadcasted_iota(jnp.int32, (t, LANE), 0)).astype(F32)
        return _split3_bf16(sign * slope * pos)

    @pl.when(qi == 0)
    def _():
        def build(b, carry):
            rows = pl.ds(pl.multiple_of(b * t, t), t)
            kaug_scr[rows, :LANE] = k_ref[rows, :]
            kaug_scr[rows, LANE:] = _bias_lanes(pos_split(b, 1.0), 1.0, split_first=True)
            return carry
        lax.fori_loop(0, nblk, build, 0)

    q = q_ref[...]
    lane = lax.broadcasted_iota(jnp.int32, q.shape, 1)
    zero = jnp.zeros_like(q)
    q_bias = _bias_lanes(pos_split(qi, -1.0), 1.0, split_first=False)
    q1_aug = jnp.concatenate([jnp.where(lane < DIFF_HALF, q, zero), q_bias], axis=1)
    q2_aug = jnp.concatenate([jnp.where(lane >= DIFF_HALF, q, zero), q_bias], axis=1)
    for m_scr, a_scr in ((m1_scr, a1_scr), (m2_scr, a2_scr)):
        m_scr[...] = jnp.full_like(m_scr, -jnp.inf)
        a_scr[...] = jnp.zeros_like(a_scr)

    def issue_scores(j, slot):
        kb = kaug_scr[pl.ds(pl.multiple_of(j * t, t), t), :]
        s_scr[slot, 0] = _dot_nt(kb, q1_aug)
        s_scr[slot, 1] = _dot_nt(kb, q2_aug)

    def consume(j, slot, masked):
        vt = vt_ref[:, pl.ds(pl.multiple_of(j * t, t), t)]
        for idx, (m_scr, a_scr) in enumerate(((m1_scr, a1_scr), (m2_scr, a2_scr))):
            s = s_scr[slot, idx]
            if masked:
                kpos, qpos = _kq_iotas(t)
                ahead = (kpos - qpos).astype(F32)
                s = s - jnp.where(ahead > 0.0, (2.0 * slope) * ahead, 0.0)
                s = jnp.where((kpos // CHUNK) <= (qpos // CHUNK), s, -jnp.inf)
            _softmax_step(s, vt, m_scr, a_scr)

    _pipelined_blocks(qi, issue_scores, consume)
    a1 = a1_scr[...]
    a2 = a2_scr[...]
    o_t = (a1[:HEAD_DIM] / a1[HEAD_DIM:HEAD_DIM + 1]
           - lam * (a2[:HEAD_DIM] / a2[HEAD_DIM:HEAD_DIM + 1]))
    _finish_head(o_t, g_ref, o_ref)


def _sb_kernel(q_ref, k_ref, vt_ref, g_ref, o_ref, s_scr, run_scr, acc_scr, *, t, tc):
    qi = pl.program_id(1)
    q = q_ref[...]
    run_scr[...] = jnp.zeros_like(run_scr)
    acc_scr[...] = jnp.zeros_like(acc_scr)
    ur = lax.broadcasted_iota(jnp.int32, (tc, tc), 0)
    uc = lax.broadcasted_iota(jnp.int32, (tc, tc), 1)
    later = jnp.concatenate([(uc > ur).astype(BF16), jnp.ones((BF16_ROWS, tc), BF16)], axis=0)
    nchunk = t // tc

    def issue_scores(j, slot):
        s_scr[slot] = _dot_nt(k_ref[pl.ds(pl.multiple_of(j * t, t), t), :], q)

    def consume(j, slot, masked):
        z = s_scr[slot]
        log_beta = jnp.minimum(z, 0.0) - jnp.log2(1.0 + jnp.exp2(-jnp.abs(z)))
        log_keep = log_beta - z
        if masked:
            kpos, qpos = _kq_iotas(t)
            strict = kpos < qpos
            log_keep = jnp.where(strict, log_keep, 0.0)
        run = run_scr[...]
        acc = acc_scr[...]
        for ci in range(nchunk - 1, -1, -1):
            sl = slice(ci * tc, (ci + 1) * tc)
            sums = _dot(later, log_keep[sl].astype(BF16))
            a = jnp.exp2(log_beta[sl] + sums[:tc])
            if masked:
                a = jnp.where(strict[sl], a, 0.0)
            cols = pl.ds(pl.multiple_of(j * t + ci * tc, tc), tc)
            acc = acc + jnp.exp2(run) * _dot(vt_ref[:, cols], a.astype(BF16))
            run = run + sums[tc:tc + 1]
        acc_scr[...] = acc
        run_scr[...] = run

    issue_scores(qi, 0)
    issue_scores(jnp.maximum(qi - 1, 0), 1)
    consume(qi, 0, True)

    def rest(p, carry):
        issue_scores(jnp.maximum(qi - 2 * p - 2, 0), 0)
        consume(qi - 2 * p - 1, 1, False)
        issue_scores(jnp.maximum(qi - 2 * p - 3, 0), 1)
        consume(qi - 2 * p - 2, 0, False)
        return carry

    lax.fori_loop(0, qi // 2, rest, 0)

    @pl.when(qi % 2 == 1)
    def _():
        consume(0, 1, False)
    _finish_head(acc_scr[...], g_ref, o_ref)


def _attn_specs(s, t, qc, kc, vc):
    return [
        pl.BlockSpec((t, LANE), lambda h, i: (i, qc + h)),
        pl.BlockSpec((s, LANE), lambda h, i: (0, kc + h)),
        pl.BlockSpec((LANE, s), lambda h, i: (vc + h, 0)),
    ]


_ATTN_PARAMS = pltpu.CompilerParams(
    dimension_semantics=("arbitrary", "arbitrary"), vmem_limit_bytes=VMEM_LIMIT)
_GCOL_SPEC = pl.BlockSpec((LANE, 1), lambda h, i: (0, 0))


def _fox(qk, vt, dtok, gcol, *, t):
    s = qk.shape[0]
    return pl.pallas_call(
        functools.partial(_fox_kernel, t=t, nblk=s // t),
        grid=(H_FOX, s // t),
        in_specs=_attn_specs(s, t, QC, KC, VC) + [
            pl.BlockSpec((s, LANE), lambda h, i: (0, 0)),
            _GCOL_SPEC,
        ],
        out_specs=pl.BlockSpec((t, LANE), lambda h, i: (i, h)),
        out_shape=jax.ShapeDtypeStruct((s, H_FOX * LANE), BF16),
        scratch_shapes=[
            pltpu.VMEM((s, 2 * LANE), BF16),
            pltpu.VMEM((2, t, t), F32),
            pltpu.VMEM((1, t), F32),
            pltpu.VMEM((HEAD_DIM + BF16_ROWS, t), F32),
        ],
        compiler_params=_ATTN_PARAMS,
        name="fox_attn",
    )(qk, qk, vt, dtok, gcol)


def _diff(qk, vt, slopes, lam, gcol, *, t):
    s = qk.shape[0]
    row = pltpu.VMEM((1, t), F32)
    acc = pltpu.VMEM((HEAD_DIM + BF16_ROWS, t), F32)
    smem = pl.BlockSpec(memory_space=pltpu.SMEM)
    return pl.pallas_call(
        functools.partial(_diff_kernel, t=t, nblk=s // t),
        grid=(H_DIFF, s // t),
        in_specs=[smem, smem] + _attn_specs(s, t, QA, KA, VA) + [_GCOL_SPEC],
        out_specs=pl.BlockSpec((t, LANE), lambda h, i: (i, h)),
        out_shape=jax.ShapeDtypeStruct((s, H_DIFF * LANE), BF16),
        scratch_shapes=[
            pltpu.VMEM((s, 2 * LANE), BF16),
            pltpu.VMEM((2, 2, t, t), F32),
            row, acc, row, acc,
        ],
        compiler_params=_ATTN_PARAMS,
        name="diff_attn",
    )(slopes, lam, qk, qk, vt, gcol)


def _sb(qk, vt, gcol, *, t, tc):
    s = qk.shape[0]
    return pl.pallas_call(
        functools.partial(_sb_kernel, t=t, tc=tc),
        grid=(H_SB, s // t),
        in_specs=_attn_specs(s, t, QB, KB, VB) + [_GCOL_SPEC],
        out_specs=pl.BlockSpec((t, LANE), lambda h, i: (i, h)),
        out_shape=jax.ShapeDtypeStruct((s, H_SB * LANE), BF16),
        scratch_shapes=[
            pltpu.VMEM((2, t, t), F32),
            pltpu.VMEM((1, t), F32),
            pltpu.VMEM((HEAD_DIM, t), F32),
        ],
        compiler_params=_ATTN_PARAMS,
        name="sb_attn",
    )(qk, qk, vt, gcol)


def _route(logits_t, bias):
    score = [jax.nn.sigmoid(logits_t[e:e + 1, :]) for e in range(N_EXPERTS)]
    sel = [score[e] + bias[e:e + 1, :] for e in range(N_EXPERTS)]
    gscore = []
    for g in range(N_GROUPS):
        mem = sel[g * EXPERTS_PER_GROUP:(g + 1) * EXPERTS_PER_GROUP]
        best = None
        for a in range(EXPERTS_PER_GROUP):
            for b in range(a + 1, EXPERTS_PER_GROUP):
                pair = mem[a] + mem[b]
                best = pair if best is None else jnp.maximum(best, pair)
        gscore.append(best)
    gmax = functools.reduce(jnp.maximum, gscore)
    taken = None
    in_group = []
    for g in range(N_GROUPS):
        hit = gscore[g] == gmax
        if taken is not None:
            hit = hit & ~taken
        taken = hit if taken is None else taken | hit
        in_group.append(hit)
    masked = [jnp.where(in_group[e // EXPERTS_PER_GROUP], sel[e], -jnp.inf) for e in range(N_EXPERTS)]

    def first_max(vals):
        top = functools.reduce(jnp.maximum, vals)
        seen = None
        picks = []
        for v in vals:
            hit = v == top
            if seen is not None:
                hit = hit & ~seen
            seen = hit if seen is None else seen | hit
            picks.append(hit)
        return picks

    pick1 = first_max(masked)
    rest = [jnp.where(pick1[e], -jnp.inf, masked[e]) for e in range(N_EXPERTS)]
    pick2 = first_max(rest)
    chosen = [pick1[e] | pick2[e] for e in range(N_EXPERTS)]
    gate = [jnp.where(chosen[e], score[e], 0.0) for e in range(N_EXPERTS)]
    total = functools.reduce(lambda a, b: a + b, gate)
    return jnp.concatenate([gt / total for gt in gate], axis=0)


def _outproj_kernel(x_ref, oa_ref, ob_ref, oc_ref, wa_ref, wb_ref, wc_ref, g_ref,
                    wrh_ref, wrl_ref, rb_ref, h_ref, hn_ref, comb_ref, *, tm):
    h = x_ref[...] + _dot(oa_ref[...], wa_ref[...]) + _dot(ob_ref[...], wb_ref[...]) \
        + _dot(oc_ref[...], wc_ref[...])
    h_ref[...] = h
    ms = jnp.mean(h * h, axis=-1, keepdims=True)
    hn = h * lax.rsqrt(ms + EPS) * g_ref[...]
    hi, lo = _split_bf16(hn)
    hn_ref[...] = hi
    wrh = wrh_ref[...]
    logits_t = _dot_nt(wrh, hi) + _dot_nt(wrh, lo) + _dot_nt(wrl_ref[...], hi)
    comb = _route(logits_t, rb_ref[...])
    pad = jnp.zeros((LANE - N_EXPERTS, tm), F32)
    comb_ref[...] = jnp.concatenate([comb, pad], axis=0).T


def _outproj(x, oa, ob, oc, wa, wb, wc, g, wrh, wrl, rb, *, tm):
    s = x.shape[0]
    row = lambda i: (i, 0)
    fixed = lambda i: (0, 0)
    return pl.pallas_call(
        functools.partial(_outproj_kernel, tm=tm),
        grid=(s // tm,),
        in_specs=[
            pl.BlockSpec((tm, D_MODEL), row),
            pl.BlockSpec((tm, H_DIFF * LANE), row),
            pl.BlockSpec((tm, H_SB * LANE), row),
            pl.BlockSpec((tm, H_FOX * LANE), row),
            pl.BlockSpec((H_DIFF * LANE, D_MODEL), fixed, pipeline_mode=pl.Buffered(1)),
            pl.BlockSpec((H_SB * LANE, D_MODEL), fixed, pipeline_mode=pl.Buffered(1)),
            pl.BlockSpec((H_FOX * LANE, D_MODEL), fixed, pipeline_mode=pl.Buffered(1)),
            pl.BlockSpec((1, D_MODEL), fixed),
            pl.BlockSpec((N_EXPERTS, D_MODEL), fixed),
            pl.BlockSpec((N_EXPERTS, D_MODEL), fixed),
            pl.BlockSpec((N_EXPERTS, 1), fixed),
        ],
        out_specs=[
            pl.BlockSpec((tm, D_MODEL), row),
            pl.BlockSpec((tm, D_MODEL), row),
            pl.BlockSpec((tm, LANE), row),
        ],
        out_shape=[
            jax.ShapeDtypeStruct((s, D_MODEL), F32),
            jax.ShapeDtypeStruct((s, D_MODEL), BF16),
            jax.ShapeDtypeStruct((s, LANE), F32),
        ],
        compiler_params=pltpu.CompilerParams(
            dimension_semantics=("arbitrary",), vmem_limit_bytes=VMEM_LIMIT),
        name="outproj_router",
    )(x, oa, ob, oc, wa, wb, wc, g, wrh, wrl, rb)


def _moe_kernel(h_ref, hn_ref, comb_ref, wg_ref, wu_ref, wd_ref, o_ref):
    e = pl.program_id(1)

    @pl.when(e == 0)
    def _():
        o_ref[...] = h_ref[...]

    hn = hn_ref[...]
    comb = comb_ref[...]
    lane = lax.broadcasted_iota(jnp.int32, comb.shape, 1)
    weight = jnp.sum(jnp.where(lane == e, comb, 0.0), axis=-1, keepdims=True)
    gate = _dot(hn, wg_ref[...])
    up = _dot(hn, wu_ref[...])
    act = gate * jax.nn.sigmoid(gate) * up * weight
    o_ref[...] += _dot(act.astype(BF16), wd_ref[...])


def _moe(h, hn, comb, wg, wu, wd, *, tm):
    s = h.shape[0]
    row = lambda i, e: (i, 0)
    return pl.pallas_call(
        _moe_kernel,
        grid=(s // tm, N_EXPERTS),
        in_specs=[
            pl.BlockSpec((tm, D_MODEL), row),
            pl.BlockSpec((tm, D_MODEL), row),
            pl.BlockSpec((tm, LANE), row),
            pl.BlockSpec((None, D_MODEL, D_EXPERT), lambda i, e: (e, 0, 0)),
            pl.BlockSpec((None, D_MODEL, D_EXPERT), lambda i, e: (e, 0, 0)),
            pl.BlockSpec((None, D_EXPERT, D_MODEL), lambda i, e: (e, 0, 0)),
        ],
        out_specs=pl.BlockSpec((tm, D_MODEL), row),
        out_shape=jax.ShapeDtypeStruct((s, D_MODEL), F32),
        compiler_params=pltpu.CompilerParams(
            dimension_semantics=("arbitrary", "arbitrary"), vmem_limit_bytes=VMEM_LIMIT),
        name="moe_experts",
    )(h, hn, comb, wg, wu, wd)


def _proj_epilogue_tables(qnorm_diff, knorm_diff, qnorm_fox, knorm_fox):
    ones = jnp.ones((HEAD_DIM,), F32)
    normed = jnp.full((HEAD_DIM,), EPS, F32)
    plain = ones
    half = jnp.arange(HEAD_DIM) // DIFF_HALF
    g_diff = (half[:, None] == half[None, :]).astype(F32) / DIFF_HALF
    g_full = jnp.full((HEAD_DIM, HEAD_DIM), 1.0 / HEAD_DIM, F32)
    g_none = jnp.zeros((HEAD_DIM, HEAD_DIM), F32)
    diff_scale = DIFF_HALF ** -0.5 * LOG2E
    full_scale = HEAD_DIM ** -0.5 * LOG2E
    groups = [
        (H_DIFF, g_diff, normed, jnp.tile(qnorm_diff, 2) * diff_scale),
        (H_DIFF, g_diff, normed, jnp.tile(knorm_diff, 2)),
        (H_SB, g_none, plain, ones * full_scale),
        (H_SB, g_none, plain, ones),
        (H_FOX, g_full, normed, qnorm_fox * full_scale),
        (H_FOX, g_full, normed, knorm_fox),
    ]
    gmat = jnp.concatenate([jnp.broadcast_to(gm, (n, HEAD_DIM, HEAD_DIM)) for n, gm, _, _ in groups])
    eps = jnp.concatenate([jnp.tile(ep, n) for n, _, ep, _ in groups])[None, :]
    gain = jnp.concatenate([jnp.tile(gn, n) for n, _, _, gn in groups])[None, :]
    return gmat.astype(BF16), eps, gain


def _split_in_weights(w):
    wa, wb, wc = H_DIFF * HEAD_DIM, H_SB * HEAD_DIM, H_FOX * HEAD_DIM
    pieces = []
    off = 0
    for width in (wa, wa, wa, wb, wb, wb, wc, wc, wc):
        pieces.append(w[:, off:off + width])
        off += width
    qa, ka, va, qb, kb, vb, qc, kc, vc = pieces
    return (jnp.concatenate([qa, ka, qb, kb, qc, kc], axis=1),
            jnp.concatenate([va, vb, vc], axis=1), w[:, off:])


def _tiles(s):
    return dict(proj_tm=min(512, s), proj_tn=512, attn_t=min(512, s), sb_tc=256,
                out_tm=min(512, s), moe_tm=min(512, s))


def kernel(x, attn_norm, w_in, b_forget, qnorm_diff, knorm_diff, lam_q1, lam_k1, lam_q2, lam_k2,
           subln_diff, onorm_sb, qnorm_fox, knorm_fox, onorm_fox, w_out, ffn_norm,
           w_router, router_bias, w_gate, w_up, w_down):
    b, s, d = x.shape
    assert b == 1 and d == D_MODEL
    depth = w_in.shape[0]
    tl = _tiles(s)
    t = tl["attn_t"]
    slopes = jnp.exp2(-8.0 * jnp.arange(1, H_DIFF + 1, dtype=F32) / H_DIFF) * LOG2E
    wr_t = w_router.T.astype(F32)
    wrh = wr_t.astype(BF16)
    wrl = (wr_t - wrh.astype(F32)).astype(BF16)
    rb = router_bias.astype(F32)[:, None]
    xs = x[0]
    w_in_b = lax.optimization_barrier(w_in.astype(BF16))
    for l in range(depth):
        w_qk, w_v, w_fc = _split_in_weights(w_in_b[l])
        wfc_t = jnp.zeros((LANE, D_MODEL), BF16).at[:H_FOX].set(w_fc.T)
        bfc = jnp.zeros((LANE, 1), F32).at[:H_FOX, 0].set(b_forget[l])
        gmat, eps, gain = _proj_epilogue_tables(qnorm_diff[l], knorm_diff[l], qnorm_fox[l], knorm_fox[l])
        qk, xn, dtok = _proj_qk(xs, attn_norm[l][None, :], w_qk, wfc_t, bfc, gmat, eps, gain,
                                tm=tl["proj_tm"], tn=tl["proj_tn"])
        vt = _proj_v(xn, w_v.T, tm=tl["proj_tm"], tv=512)

        lam_init = 0.8 - 0.6 * math.exp(-0.3 * l)
        lam = (jnp.exp(jnp.sum(lam_q1[l] * lam_k1[l])) - jnp.exp(jnp.sum(lam_q2[l] * lam_k2[l]))
               + lam_init).reshape(1).astype(F32)
        oa = _diff(qk, vt, slopes, lam, (subln_diff[l] * (1.0 - lam_init))[:, None], t=t)
        ob = _sb(qk, vt, onorm_sb[l][:, None], t=t, tc=min(tl["sb_tc"], t))
        oc = _fox(qk, vt, dtok, onorm_fox[l][:, None], t=t)

        wo = w_out[l].astype(BF16)
        na, nb = H_DIFF * LANE, (H_DIFF + H_SB) * LANE
        h, hn, comb = _outproj(xs, oa, ob, oc, wo[:na], wo[na:nb], wo[nb:], ffn_norm[l][None, :],
                               wrh, wrl, rb, tm=tl["out_tm"])
        xs = _moe(h, hn, comb, w_gate[l].astype(BF16), w_up[l].astype(BF16), w_down[l].astype(BF16),
                  tm=tl["moe_tm"])
    return xs[None]
```

```python
import functools
import math

import jax
import jax.numpy as jnp
from jax import lax
from jax.experimental import pallas as pl
from jax.experimental.pallas import tpu as pltpu

F32 = jnp.float32
BF16 = jnp.bfloat16

D_MODEL = 2048
HEAD_DIM = 128
LANE = 128
BF16_ROWS = 16
N_HEADS = 16
H_SB = N_HEADS // 3
H_FOX = N_HEADS // 3
H_DIFF = N_HEADS - H_SB - H_FOX
DIFF_HALF = HEAD_DIM // 2
CHUNK = 64
N_EXPERTS = 16
N_GROUPS = 4
EXPERTS_PER_GROUP = N_EXPERTS // N_GROUPS
D_EXPERT = D_MODEL // 4
EPS = 1e-6
LOG2E = 1.4426950408889634

QA, KA = 0, H_DIFF
QB, KB = 2 * H_DIFF, 2 * H_DIFF + H_SB
QC, KC = 2 * H_DIFF + 2 * H_SB, 2 * H_DIFF + 2 * H_SB + H_FOX
N_QK = 2 * N_HEADS * HEAD_DIM
VA, VB, VC = 0, H_DIFF, H_DIFF + H_SB
N_V = N_HEADS * HEAD_DIM

VMEM_LIMIT = 48 * 1024 * 1024

_NT = (((1,), (1,)), ((), ()))


def _dot(a, b):
    return jnp.dot(a, b, preferred_element_type=F32)


def _dot_nt(a, b):
    return lax.dot_general(a, b, _NT, preferred_element_type=F32)


def _log_sigmoid(x):
    return jnp.minimum(x, 0.0) - jnp.log1p(jnp.exp(-jnp.abs(x)))


def _split_bf16(x):
    hi = x.astype(BF16)
    lo = (x - hi.astype(F32)).astype(BF16)
    return hi, lo


def _split3_bf16(x):
    hi = x.astype(BF16)
    r = x - hi.astype(F32)
    mid = r.astype(BF16)
    lo = (r - mid.astype(F32)).astype(BF16)
    return hi, mid, lo


def _proj_kernel(x_ref, g_ref, wqk_ref, wvt_ref, wfc_ref, bfc_ref, gmat_ref, eps_ref, gain_ref,
                 qk_ref, vt_ref, dtok_ref, carry_scr, *, tm, tn):
    x = x_ref[...]
    ms = jnp.mean(x * x, axis=-1, keepdims=True)
    xn = (x * lax.rsqrt(ms + EPS) * g_ref[...]).astype(BF16)

    @pl.when(pl.program_id(0) == 0)
    def _():
        carry_scr[...] = jnp.zeros_like(carry_scr)

    lf = _log_sigmoid(_dot_nt(wfc_ref[...], xn) + bfc_ref[...])
    hi, lo = _split_bf16(lf)
    r = lax.broadcasted_iota(jnp.int32, (tm, tm), 0)
    c = lax.broadcasted_iota(jnp.int32, (tm, tm), 1)
    tri = (r <= c).astype(BF16)
    cum = _dot(hi, tri) + _dot(lo, tri) + carry_scr[...]
    carry_scr[...] = cum[:, tm - 1:tm]
    dtok_ref[...] = (cum * LOG2E).T

    def project(j):
        return _dot(xn, wqk_ref[:, pl.ds(pl.multiple_of(j * tn, tn), tn)])

    def epilogue(j, acc):
        for c in range(tn // LANE):
            cols = pl.ds(pl.multiple_of(j * tn + c * LANE, LANE), LANE)
            y = acc[:, c * LANE:(c + 1) * LANE]
            ss = _dot((y * y).astype(BF16), gmat_ref[j * (tn // LANE) + c])
            qk_ref[:, cols] = (y * lax.rsqrt(ss + eps_ref[:, cols]) * gain_ref[:, cols]).astype(BF16)

    def qk_block(j, acc):
        nxt = project(j + 1)
        epilogue(j, acc)
        return nxt

    last = N_QK // tn - 1
    epilogue(last, lax.fori_loop(0, last, qk_block, project(0)))

    def v_block(j, carry):
        rows = pl.ds(pl.multiple_of(j * tn, tn), tn)
        vt_ref[rows, :] = _dot_nt(wvt_ref[rows, :], xn).astype(BF16)
        return carry

    lax.fori_loop(0, N_V // tn, v_block, 0)


def _proj(x, g, wqk, wv_t, wfc_t, bfc, gmat, eps, gain, *, tm, tn):
    s = x.shape[0]
    fixed2 = lambda i: (0, 0)
    resident = pl.Buffered(1)
    return pl.pallas_call(
        functools.partial(_proj_kernel, tm=tm, tn=tn),
        grid=(s // tm,),
        in_specs=[
            pl.BlockSpec((tm, D_MODEL), lambda i: (i, 0)),
            pl.BlockSpec((1, D_MODEL), fixed2),
            pl.BlockSpec((D_MODEL, N_QK), fixed2, pipeline_mode=resident),
            pl.BlockSpec((N_V, D_MODEL), fixed2, pipeline_mode=resident),
            pl.BlockSpec((LANE, D_MODEL), fixed2),
            pl.BlockSpec((LANE, 1), fixed2),
            pl.BlockSpec((N_QK // LANE, LANE, LANE), lambda i: (0, 0, 0), pipeline_mode=resident),
            pl.BlockSpec((1, N_QK), fixed2),
            pl.BlockSpec((1, N_QK), fixed2),
        ],
        out_specs=[
            pl.BlockSpec((tm, N_QK), lambda i: (i, 0)),
            pl.BlockSpec((N_V, tm), lambda i: (0, i)),
            pl.BlockSpec((tm, LANE), lambda i: (i, 0)),
        ],
        out_shape=[
            jax.ShapeDtypeStruct((s, N_QK), BF16),
            jax.ShapeDtypeStruct((N_V, s), BF16),
            jax.ShapeDtypeStruct((s, LANE), F32),
        ],
        scratch_shapes=[pltpu.VMEM((LANE, 1), F32)],
        compiler_params=pltpu.CompilerParams(
            dimension_semantics=("arbitrary",), vmem_limit_bytes=VMEM_LIMIT),
        name="proj",
    )(x, g, wqk, wv_t, wfc_t, bfc, gmat, eps, gain)


def _bias_lanes(split, ones_value, *, split_first):
    hi, mid, lo = split
    lane = lax.broadcasted_iota(jnp.int32, hi.shape, 1)
    s0, c0 = (0, 3) if split_first else (3, 0)
    const = jnp.where((lane >= c0) & (lane < c0 + 3), ones_value, 0.0).astype(BF16)
    return jnp.where(lane == s0, hi, jnp.where(lane == s0 + 1, mid, jnp.where(lane == s0 + 2, lo, const)))


def _kq_iotas(t):
    return (lax.broadcasted_iota(jnp.int32, (t, t), 0),
            lax.broadcasted_iota(jnp.int32, (t, t), 1))


def _fold_rows(x, op):
    while x.shape[0] > 8:
        half = x.shape[0] // 2
        x = op(x[:half], x[half:])
    return x


def _softmax_step(s, vt, m_scr, acc_scr):
    m_old = m_scr[...]
    m_new = jnp.maximum(m_old, jnp.max(_fold_rows(s, jnp.maximum), axis=0, keepdims=True))
    alpha = jnp.exp2(m_old - m_new)
    p = jnp.exp2(s - m_new).astype(BF16)
    vt_aug = jnp.concatenate([vt, jnp.ones((BF16_ROWS, vt.shape[1]), BF16)], axis=0)
    acc_scr[...] = alpha * acc_scr[...] + _dot(vt_aug, p)
    m_scr[...] = m_new


def _finish_head(o_t, gcol_ref, o_ref):
    ms = jnp.mean(o_t * o_t, axis=0, keepdims=True)
    o_ref[...] = (o_t * lax.rsqrt(ms + EPS) * gcol_ref[...]).T.astype(BF16)


def _pipelined_blocks(qi, issue_scores, consume):
    issue_scores(0, 0)

    def body(p, carry):
        issue_scores(2 * p + 1, 1)
        consume(2 * p, 0, False)
        issue_scores(2 * p + 2, 0)
        consume(2 * p + 1, 1, False)
        return carry

    lax.fori_loop(0, qi // 2, body, 0)

    @pl.when(qi % 2 == 0)
    def _():
        consume(qi, 0, True)

    @pl.when(qi % 2 == 1)
    def _():
        issue_scores(qi, 1)
        consume(qi - 1, 0, False)
        consume(qi, 1, True)


def _fox_kernel(q_ref, k_ref, vt_ref, d_ref, g_ref, o_ref, kaug_scr, s_scr, m_scr, acc_scr, *, t, nblk):
    h = pl.program_id(0)
    qi = pl.program_id(1)
    pick = (lax.broadcasted_iota(jnp.int32, (LANE, LANE), 0) == h).astype(BF16)

    def head_bias(rows):
        return tuple(_dot(p, pick).astype(BF16) for p in _split3_bf16(d_ref[rows, :]))

    @pl.when(qi == 0)
    def _():
        def build(b, carry):
            rows = pl.ds(pl.multiple_of(b * t, t), t)
            kaug_scr[rows, :LANE] = k_ref[rows, :]
            kaug_scr[rows, LANE:] = _bias_lanes(head_bias(rows), 1.0, split_first=True)
            return carry
        lax.fori_loop(0, nblk, build, 0)

    qrows = pl.ds(pl.multiple_of(qi * t, t), t)
    q_aug = jnp.concatenate([q_ref[...], _bias_lanes(head_bias(qrows), -1.0, split_first=False)], axis=1)
    m_scr[...] = jnp.full_like(m_scr, -jnp.inf)
    acc_scr[...] = jnp.zeros_like(acc_scr)

    def issue_scores(j, slot):
        s_scr[slot] = _dot_nt(kaug_scr[pl.ds(pl.multiple_of(j * t, t), t), :], q_aug)

    def consume(j, slot, masked):
        s = s_scr[slot]
        if masked:
            kpos, qpos = _kq_iotas(t)
            s = jnp.where(kpos <= qpos, s, -jnp.inf)
        _softmax_step(s, vt_ref[:, pl.ds(pl.multiple_of(j * t, t), t)], m_scr, acc_scr)

    _pipelined_blocks(qi, issue_scores, consume)
    acc = acc_scr[...]
    _finish_head(acc[:HEAD_DIM] / acc[HEAD_DIM:HEAD_DIM + 1], g_ref, o_ref)


def _diff_kernel(slope_ref, lam_ref, q_ref, k_ref, vt_ref, g_ref, o_ref,
                 kaug_scr, s_scr, m1_scr, a1_scr, m2_scr, a2_scr, *, t, nblk):
    h = pl.program_id(0)
    qi = pl.program_id(1)
    slope = slope_ref[h]
    lam = lam_ref[0]

    def pos_split(block, sign):
        pos = (block * t + lax.broadcasted_iota(jnp.int32, (t, LANE), 0)).astype(F32)
        return _split3_bf16(sign * slope * pos)

    @pl.when(qi == 0)
    def _():
        def build(b, carry):
            rows = pl.ds(pl.multiple_of(b * t, t), t)
            kaug_scr[rows, :LANE] = k_ref[rows, :]
            kaug_scr[rows, LANE:] = _bias_lanes(pos_split(b, 1.0), 1.0, split_first=True)
            return carry
        lax.fori_loop(0, nblk, build, 0)

    q = q_ref[...]
    lane = lax.broadcasted_iota(jnp.int32, q.shape, 1)
    zero = jnp.zeros_like(q)
    q_bias = _bias_lanes(pos_split(qi, -1.0), 1.0, split_first=False)
    q1_aug = jnp.concatenate([jnp.where(lane < DIFF_HALF, q, zero), q_bias], axis=1)
    q2_aug = jnp.concatenate([jnp.where(lane >= DIFF_HALF, q, zero), q_bias], axis=1)
    for m_scr, a_scr in ((m1_scr, a1_scr), (m2_scr, a2_scr)):
        m_scr[...] = jnp.full_like(m_scr, -jnp.inf)
        a_scr[...] = jnp.zeros_like(a_scr)

    def issue_scores(j, slot):
        kb = kaug_scr[pl.ds(pl.multiple_of(j * t, t), t), :]
        s_scr[slot, 0] = _dot_nt(kb, q1_aug)
        s_scr[slot, 1] = _dot_nt(kb, q2_aug)

    def consume(j, slot, masked):
        vt = vt_ref[:, pl.ds(pl.multiple_of(j * t, t), t)]
        for idx, (m_scr, a_scr) in enumerate(((m1_scr, a1_scr), (m2_scr, a2_scr))):
            s = s_scr[slot, idx]
            if masked:
                kpos, qpos = _kq_iotas(t)
                ahead = (kpos - qpos).astype(F32)
                s = s - jnp.where(ahead > 0.0, (2.0 * slope) * ahead, 0.0)
                s = jnp.where((kpos // CHUNK) <= (qpos // CHUNK), s, -jnp.inf)
            _softmax_step(s, vt, m_scr, a_scr)

    _pipelined_blocks(qi, issue_scores, consume)
    a1 = a1_scr[...]
    a2 = a2_scr[...]
    o_t = (a1[:HEAD_DIM] / a1[HEAD_DIM:HEAD_DIM + 1]
           - lam * (a2[:HEAD_DIM] / a2[HEAD_DIM:HEAD_DIM + 1]))
    _finish_head(o_t, g_ref, o_ref)


def _sb_kernel(q_ref, k_ref, vt_ref, g_ref, o_ref, s_scr, run_scr, acc_scr, *, t, tc):
    qi = pl.program_id(1)
    q = q_ref[...]
    run_scr[...] = jnp.zeros_like(run_scr)
    acc_scr[...] = jnp.zeros_like(acc_scr)
    ur = lax.broadcasted_iota(jnp.int32, (tc, tc), 0)
    uc = lax.broadcasted_iota(jnp.int32, (tc, tc), 1)
    later = jnp.concatenate([(uc > ur).astype(BF16), jnp.ones((BF16_ROWS, tc), BF16)], axis=0)
    nchunk = t // tc

    def issue_scores(j, slot):
        s_scr[slot] = _dot_nt(k_ref[pl.ds(pl.multiple_of(j * t, t), t), :], q)

    def consume(j, slot, masked):
        z = s_scr[slot]
        log_beta = jnp.minimum(z, 0.0) - jnp.log2(1.0 + jnp.exp2(-jnp.abs(z)))
        log_keep = log_beta - z
        if masked:
            kpos, qpos = _kq_iotas(t)
            strict = kpos < qpos
            log_keep = jnp.where(strict, log_keep, 0.0)
        run = run_scr[...]
        acc = acc_scr[...]
        for ci in range(nchunk - 1, -1, -1):
            sl = slice(ci * tc, (ci + 1) * tc)
            sums = _dot(later, log_keep[sl].astype(BF16))
            a = jnp.exp2(log_beta[sl] + sums[:tc])
            if masked:
                a = jnp.where(strict[sl], a, 0.0)
            cols = pl.ds(pl.multiple_of(j * t + ci * tc, tc), tc)
            acc = acc + jnp.exp2(run) * _dot(vt_ref[:, cols], a.astype(BF16))
            run = run + sums[tc:tc + 1]
        acc_scr[...] = acc
        run_scr[...] = run

    issue_scores(qi, 0)
    issue_scores(jnp.maximum(qi - 1, 0), 1)
    consume(qi, 0, True)

    def rest(p, carry):
        issue_scores(jnp.maximum(qi - 2 * p - 2, 0), 0)
        consume(qi - 2 * p - 1, 1, False)
        issue_scores(jnp.maximum(qi - 2 * p - 3, 0), 1)
        consume(qi - 2 * p - 2, 0, False)
        return carry

    lax.fori_loop(0, qi // 2, rest, 0)

    @pl.when(qi % 2 == 1)
    def _():
        consume(0, 1, False)
    _finish_head(acc_scr[...], g_ref, o_ref)


def _attn_specs(s, t, qc, kc, vc):
    return [
        pl.BlockSpec((t, LANE), lambda h, i: (i, qc + h)),
        pl.BlockSpec((s, LANE), lambda h, i: (0, kc + h)),
        pl.BlockSpec((LANE, s), lambda h, i: (vc + h, 0)),
    ]


_ATTN_PARAMS = pltpu.CompilerParams(
    dimension_semantics=("arbitrary", "arbitrary"), vmem_limit_bytes=VMEM_LIMIT)
_GCOL_SPEC = pl.BlockSpec((LANE, 1), lambda h, i: (0, 0))


def _fox(qk, vt, dtok, gcol, *, t):
    s = qk.shape[0]
    return pl.pallas_call(
        functools.partial(_fox_kernel, t=t, nblk=s // t),
        grid=(H_FOX, s // t),
        in_specs=_attn_specs(s, t, QC, KC, VC) + [
            pl.BlockSpec((s, LANE), lambda h, i: (0, 0)),
            _GCOL_SPEC,
        ],
        out_specs=pl.BlockSpec((t, LANE), lambda h, i: (i, h)),
        out_shape=jax.ShapeDtypeStruct((s, H_FOX * LANE), BF16),
        scratch_shapes=[
            pltpu.VMEM((s, 2 * LANE), BF16),
            pltpu.VMEM((2, t, t), F32),
            pltpu.VMEM((1, t), F32),
            pltpu.VMEM((HEAD_DIM + BF16_ROWS, t), F32),
        ],
        compiler_params=_ATTN_PARAMS,
        name="fox_attn",
    )(qk, qk, vt, dtok, gcol)


def _diff(qk, vt, slopes, lam, gcol, *, t):
    s = qk.shape[0]
    row = pltpu.VMEM((1, t), F32)
    acc = pltpu.VMEM((HEAD_DIM + BF16_ROWS, t), F32)
    smem = pl.BlockSpec(memory_space=pltpu.SMEM)
    return pl.pallas_call(
        functools.partial(_diff_kernel, t=t, nblk=s // t),
        grid=(H_DIFF, s // t),
        in_specs=[smem, smem] + _attn_specs(s, t, QA, KA, VA) + [_GCOL_SPEC],
        out_specs=pl.BlockSpec((t, LANE), lambda h, i: (i, h)),
        out_shape=jax.ShapeDtypeStruct((s, H_DIFF * LANE), BF16),
        scratch_shapes=[
            pltpu.VMEM((s, 2 * LANE), BF16),
            pltpu.VMEM((2, 2, t, t), F32),
            row, acc, row, acc,
        ],
        compiler_params=_ATTN_PARAMS,
        name="diff_attn",
    )(slopes, lam, qk, qk, vt, gcol)


def _sb(qk, vt, gcol, *, t, tc):
    s = qk.shape[0]
    return pl.pallas_call(
        functools.partial(_sb_kernel, t=t, tc=tc),
        grid=(H_SB, s // t),
        in_specs=_attn_specs(s, t, QB, KB, VB) + [_GCOL_SPEC],
        out_specs=pl.BlockSpec((t, LANE), lambda h, i: (i, h)),
        out_shape=jax.ShapeDtypeStruct((s, H_SB * LANE), BF16),
        scratch_shapes=[
            pltpu.VMEM((2, t, t), F32),
            pltpu.VMEM((1, t), F32),
            pltpu.VMEM((HEAD_DIM, t), F32),
        ],
        compiler_params=_ATTN_PARAMS,
        name="sb_attn",
    )(qk, qk, vt, gcol)


def _route(logits_t, bias):
    score = [jax.nn.sigmoid(logits_t[e:e + 1, :]) for e in range(N_EXPERTS)]
    sel = [score[e] + bias[e:e + 1, :] for e in range(N_EXPERTS)]
    gscore = []
    for g in range(N_GROUPS):
        mem = sel[g * EXPERTS_PER_GROUP:(g + 1) * EXPERTS_PER_GROUP]
        best = None
        for a in range(EXPERTS_PER_GROUP):
            for b in range(a + 1, EXPERTS_PER_GROUP):
                pair = mem[a] + mem[b]
                best = pair if best is None else jnp.maximum(best, pair)
        gscore.append(best)
    gmax = functools.reduce(jnp.maximum, gscore)
    taken = None
    in_group = []
    for g in range(N_GROUPS):
        hit = gscore[g] == gmax
        if taken is not None:
            hit = hit & ~taken
        taken = hit if taken is None else taken | hit
        in_group.append(hit)
    masked = [jnp.where(in_group[e // EXPERTS_PER_GROUP], sel[e], -jnp.inf) for e in range(N_EXPERTS)]

    def first_max(vals):
        top = functools.reduce(jnp.maximum, vals)
        seen = None
        picks = []
        for v in vals:
            hit = v == top
            if seen is not None:
                hit = hit & ~seen
            seen = hit if seen is None else seen | hit
            picks.append(hit)
        return picks

    pick1 = first_max(masked)
    rest = [jnp.where(pick1[e], -jnp.inf, masked[e]) for e in range(N_EXPERTS)]
    pick2 = first_max(rest)
    chosen = [pick1[e] | pick2[e] for e in range(N_EXPERTS)]
    gate = [jnp.where(chosen[e], score[e], 0.0) for e in range(N_EXPERTS)]
    total = functools.reduce(lambda a, b: a + b, gate)
    return jnp.concatenate([gt / total for gt in gate], axis=0)


def _outproj_kernel(x_ref, oa_ref, ob_ref, oc_ref, wa_ref, wb_ref, wc_ref, g_ref,
                    wrh_ref, wrl_ref, rb_ref, h_ref, hn_ref, comb_ref, *, tm):
    h = x_ref[...] + _dot(oa_ref[...], wa_ref[...]) + _dot(ob_ref[...], wb_ref[...]) \
        + _dot(oc_ref[...], wc_ref[...])
    h_ref[...] = h
    ms = jnp.mean(h * h, axis=-1, keepdims=True)
    hn = h * lax.rsqrt(ms + EPS) * g_ref[...]
    hi, lo = _split_bf16(hn)
    hn_ref[...] = hi
    wrh = wrh_ref[...]
    logits_t = _dot_nt(wrh, hi) + _dot_nt(wrh, lo) + _dot_nt(wrl_ref[...], hi)
    comb = _route(logits_t, rb_ref[...])
    pad = jnp.zeros((LANE - N_EXPERTS, tm), F32)
    comb_ref[...] = jnp.concatenate([comb, pad], axis=0).T


def _outproj(x, oa, ob, oc, wa, wb, wc, g, wrh, wrl, rb, *, tm):
    s = x.shape[0]
    row = lambda i: (i, 0)
    fixed = lambda i: (0, 0)
    return pl.pallas_call(
        functools.partial(_outproj_kernel, tm=tm),
        grid=(s // tm,),
        in_specs=[
            pl.BlockSpec((tm, D_MODEL), row),
            pl.BlockSpec((tm, H_DIFF * LANE), row),
            pl.BlockSpec((tm, H_SB * LANE), row),
            pl.BlockSpec((tm, H_FOX * LANE), row),
            pl.BlockSpec((H_DIFF * LANE, D_MODEL), fixed, pipeline_mode=pl.Buffered(1)),
            pl.BlockSpec((H_SB * LANE, D_MODEL), fixed, pipeline_mode=pl.Buffered(1)),
            pl.BlockSpec((H_FOX * LANE, D_MODEL), fixed, pipeline_mode=pl.Buffered(1)),
            pl.BlockSpec((1, D_MODEL), fixed),
            pl.BlockSpec((N_EXPERTS, D_MODEL), fixed),
            pl.BlockSpec((N_EXPERTS, D_MODEL), fixed),
            pl.BlockSpec((N_EXPERTS, 1), fixed),
        ],
        out_specs=[
            pl.BlockSpec((tm, D_MODEL), row),
            pl.BlockSpec((tm, D_MODEL), row),
            pl.BlockSpec((tm, LANE), row),
        ],
        out_shape=[
            jax.ShapeDtypeStruct((s, D_MODEL), F32),
            jax.ShapeDtypeStruct((s, D_MODEL), BF16),
            jax.ShapeDtypeStruct((s, LANE), F32),
        ],
        compiler_params=pltpu.CompilerParams(
            dimension_semantics=("arbitrary",), vmem_limit_bytes=VMEM_LIMIT),
        name="outproj_router",
    )(x, oa, ob, oc, wa, wb, wc, g, wrh, wrl, rb)


def _moe_kernel(h_ref, hn_ref, comb_ref, wg_ref, wu_ref, wd_ref, o_ref):
    e = pl.program_id(1)

    @pl.when(e == 0)
    def _():
        o_ref[...] = h_ref[...]

    hn = hn_ref[...]
    comb = comb_ref[...]
    lane = lax.broadcasted_iota(jnp.int32, comb.shape, 1)
    weight = jnp.sum(jnp.where(lane == e, comb, 0.0), axis=-1, keepdims=True)
    gate = _dot(hn, wg_ref[...])
    up = _dot(hn, wu_ref[...])
    act = gate * jax.nn.sigmoid(gate) * up * weight
    o_ref[...] += _dot(act.astype(BF16), wd_ref[...])


def _moe(h, hn, comb, wg, wu, wd, *, tm):
    s = h.shape[0]
    row = lambda i, e: (i, 0)
    return pl.pallas_call(
        _moe_kernel,
        grid=(s // tm, N_EXPERTS),
        in_specs=[
            pl.BlockSpec((tm, D_MODEL), row),
            pl.BlockSpec((tm, D_MODEL), row),
            pl.BlockSpec((tm, LANE), row),
            pl.BlockSpec((None, D_MODEL, D_EXPERT), lambda i, e: (e, 0, 0)),
            pl.BlockSpec((None, D_MODEL, D_EXPERT), lambda i, e: (e, 0, 0)),
            pl.BlockSpec((None, D_EXPERT, D_MODEL), lambda i, e: (e, 0, 0)),
        ],
        out_specs=pl.BlockSpec((tm, D_MODEL), row),
        out_shape=jax.ShapeDtypeStruct((s, D_MODEL), F32),
        compiler_params=pltpu.CompilerParams(
            dimension_semantics=("arbitrary", "arbitrary"), vmem_limit_bytes=VMEM_LIMIT),
        name="moe_experts",
    )(h, hn, comb, wg, wu, wd)


def _proj_epilogue_tables(qnorm_diff, knorm_diff, qnorm_fox, knorm_fox):
    ones = jnp.ones((HEAD_DIM,), F32)
    normed = jnp.full((HEAD_DIM,), EPS, F32)
    plain = ones
    half = jnp.arange(HEAD_DIM) // DIFF_HALF
    g_diff = (half[:, None] == half[None, :]).astype(F32) / DIFF_HALF
    g_full = jnp.full((HEAD_DIM, HEAD_DIM), 1.0 / HEAD_DIM, F32)
    g_none = jnp.zeros((HEAD_DIM, HEAD_DIM), F32)
    diff_scale = DIFF_HALF ** -0.5 * LOG2E
    full_scale = HEAD_DIM ** -0.5 * LOG2E
    groups = [
        (H_DIFF, g_diff, normed, jnp.tile(qnorm_diff, 2) * diff_scale),
        (H_DIFF, g_diff, normed, jnp.tile(knorm_diff, 2)),
        (H_SB, g_none, plain, ones * full_scale),
        (H_SB, g_none, plain, ones),
        (H_FOX, g_full, normed, qnorm_fox * full_scale),
        (H_FOX, g_full, normed, knorm_fox),
    ]
    gmat = jnp.concatenate([jnp.broadcast_to(gm, (n, HEAD_DIM, HEAD_DIM)) for n, gm, _, _ in groups])
    eps = jnp.concatenate([jnp.tile(ep, n) for n, _, ep, _ in groups])[None, :]
    gain = jnp.concatenate([jnp.tile(gn, n) for n, _, _, gn in groups])[None, :]
    return gmat.astype(BF16), eps, gain


def _split_in_weights(w):
    wa, wb, wc = H_DIFF * HEAD_DIM, H_SB * HEAD_DIM, H_FOX * HEAD_DIM
    pieces = []
    off = 0
    for width in (wa, wa, wa, wb, wb, wb, wc, wc, wc):
        pieces.append(w[:, off:off + width])
        off += width
    qa, ka, va, qb, kb, vb, qc, kc, vc = pieces
    return (jnp.concatenate([qa, ka, qb, kb, qc, kc], axis=1),
            jnp.concatenate([va, vb, vc], axis=1), w[:, off:])


def _tiles(s):
    return dict(proj_tm=min(256, s), proj_tn=512, attn_t=min(512, s), sb_tc=256,
                out_tm=min(512, s), moe_tm=min(512, s))


def kernel(x, attn_norm, w_in, b_forget, qnorm_diff, knorm_diff, lam_q1, lam_k1, lam_q2, lam_k2,
           subln_diff, onorm_sb, qnorm_fox, knorm_fox, onorm_fox, w_out, ffn_norm,
           w_router, router_bias, w_gate, w_up, w_down):
    b, s, d = x.shape
    assert b == 1 and d == D_MODEL
    depth = w_in.shape[0]
    tl = _tiles(s)
    t = tl["attn_t"]
    slopes = jnp.exp2(-8.0 * jnp.arange(1, H_DIFF + 1, dtype=F32) / H_DIFF) * LOG2E
    wr_t = w_router.T.astype(F32)
    wrh = wr_t.astype(BF16)
    wrl = (wr_t - wrh.astype(F32)).astype(BF16)
    rb = router_bias.astype(F32)[:, None]
    xs = x[0]
    w_in_b = lax.optimization_barrier(w_in.astype(BF16))
    for l in range(depth):
        w_qk, w_v, w_fc = _split_in_weights(w_in_b[l])
        wfc_t = jnp.zeros((LANE, D_MODEL), BF16).at[:H_FOX].set(w_fc.T)
        bfc = jnp.zeros((LANE, 1), F32).at[:H_FOX, 0].set(b_forget[l])
        gmat, eps, gain = _proj_epilogue_tables(qnorm_diff[l], knorm_diff[l], qnorm_fox[l], knorm_fox[l])
        qk, vt, dtok = _proj(xs, attn_norm[l][None, :], w_qk, w_v.T, wfc_t, bfc, gmat, eps, gain,
                             tm=tl["proj_tm"], tn=tl["proj_tn"])

        lam_init = 0.8 - 0.6 * math.exp(-0.3 * l)
        lam = (jnp.exp(jnp.sum(lam_q1[l] * lam_k1[l])) - jnp.exp(jnp.sum(lam_q2[l] * lam_k2[l]))
               + lam_init).reshape(1).astype(F32)
        oa = _diff(qk, vt, slopes, lam, (subln_diff[l] * (1.0 - lam_init))[:, None], t=t)
        ob = _sb(qk, vt, onorm_sb[l][:, None], t=t, tc=min(tl["sb_tc"], t))
        oc = _fox(qk, vt, dtok, onorm_fox[l][:, None], t=t)

        wo = w_out[l].astype(BF16)
        na, nb = H_DIFF * LANE, (H_DIFF + H_SB) * LANE
        h, hn, comb = _outproj(xs, oa, ob, oc, wo[:na], wo[na:nb], wo[nb:], ffn_norm[l][None, :],
                               wrh, wrl, rb, tm=tl["out_tm"])
        xs = _moe(h, hn, comb, w_gate[l].astype(BF16), w_up[l].astype(BF16), w_down[l].astype(BF16),
                  tm=tl["moe_tm"])
    return xs[None]
```

```python
import functools
import math

import jax
import jax.numpy as jnp
from jax import lax
from jax.experimental import pallas as pl
from jax.experimental.pallas import tpu as pltpu

F32 = jnp.float32
BF16 = jnp.bfloat16

D_MODEL = 2048
HEAD_DIM = 128
LANE = 128
BF16_ROWS = 16
N_HEADS = 16
H_SB = N_HEADS // 3
H_FOX = N_HEADS // 3
H_DIFF = N_HEADS - H_SB - H_FOX
DIFF_HALF = HEAD_DIM // 2
CHUNK = 64
N_EXPERTS = 16
N_GROUPS = 4
EXPERTS_PER_GROUP = N_EXPERTS // N_GROUPS
D_EXPERT = D_MODEL // 4
EPS = 1e-6
LOG2E = 1.4426950408889634

QA, KA = 0, H_DIFF
QB, KB = 2 * H_DIFF, 2 * H_DIFF + H_SB
QC, KC = 2 * H_DIFF + 2 * H_SB, 2 * H_DIFF + 2 * H_SB + H_FOX
N_QK = 2 * N_HEADS * HEAD_DIM
VA, VB, VC = 0, H_DIFF, H_DIFF + H_SB
N_V = N_HEADS * HEAD_DIM

VMEM_LIMIT = 48 * 1024 * 1024

_NT = (((1,), (1,)), ((), ()))


def _dot(a, b):
    return jnp.dot(a, b, preferred_element_type=F32)


def _dot_nt(a, b):
    return lax.dot_general(a, b, _NT, preferred_element_type=F32)


def _log_sigmoid(x):
    return jnp.minimum(x, 0.0) - jnp.log1p(jnp.exp(-jnp.abs(x)))


def _split_bf16(x):
    hi = x.astype(BF16)
    lo = (x - hi.astype(F32)).astype(BF16)
    return hi, lo


def _split3_bf16(x):
    hi = x.astype(BF16)
    r = x - hi.astype(F32)
    mid = r.astype(BF16)
    lo = (r - mid.astype(F32)).astype(BF16)
    return hi, mid, lo


def _proj_kernel(x_ref, g_ref, wqk_ref, wvt_ref, wfc_ref, bfc_ref, gmat_ref, eps_ref, gain_ref,
                 qk_ref, vt_ref, dtok_ref, carry_scr, *, tm, tn):
    x = x_ref[...]
    ms = jnp.mean(x * x, axis=-1, keepdims=True)
    xn = (x * lax.rsqrt(ms + EPS) * g_ref[...]).astype(BF16)

    @pl.when(pl.program_id(0) == 0)
    def _():
        carry_scr[...] = jnp.zeros_like(carry_scr)

    lf = _log_sigmoid(_dot_nt(wfc_ref[...], xn) + bfc_ref[...])
    hi, lo = _split_bf16(lf)
    r = lax.broadcasted_iota(jnp.int32, (tm, tm), 0)
    c = lax.broadcasted_iota(jnp.int32, (tm, tm), 1)
    tri = (r <= c).astype(BF16)
    cum = _dot(hi, tri) + _dot(lo, tri) + carry_scr[...]
    carry_scr[...] = cum[:, tm - 1:tm]
    dtok_ref[...] = (cum * LOG2E).T

    def project(j):
        return _dot(xn, wqk_ref[:, pl.ds(pl.multiple_of(j * tn, tn), tn)])

    def epilogue(j, acc):
        for c in range(tn // LANE):
            cols = pl.ds(pl.multiple_of(j * tn + c * LANE, LANE), LANE)
            y = acc[:, c * LANE:(c + 1) * LANE]
            ss = _dot((y * y).astype(BF16), gmat_ref[j * (tn // LANE) + c])
            qk_ref[:, cols] = (y * lax.rsqrt(ss + eps_ref[:, cols]) * gain_ref[:, cols]).astype(BF16)

    def qk_block(j, acc):
        nxt = project(j + 1)
        epilogue(j, acc)
        return nxt

    last = N_QK // tn - 1
    epilogue(last, lax.fori_loop(0, last, qk_block, project(0)))

    def v_block(j, carry):
        rows = pl.ds(pl.multiple_of(j * tn, tn), tn)
        vt_ref[rows, :] = _dot_nt(wvt_ref[rows, :], xn).astype(BF16)
        return carry

    lax.fori_loop(0, N_V // tn, v_block, 0)


def _proj(x, g, wqk, wv_t, wfc_t, bfc, gmat, eps, gain, *, tm, tn):
    s = x.shape[0]
    fixed2 = lambda i: (0, 0)
    resident = pl.Buffered(1)
    return pl.pallas_call(
        functools.partial(_proj_kernel, tm=tm, tn=tn),
        grid=(s // tm,),
        in_specs=[
            pl.BlockSpec((tm, D_MODEL), lambda i: (i, 0)),
            pl.BlockSpec((1, D_MODEL), fixed2),
            pl.BlockSpec((D_MODEL, N_QK), fixed2, pipeline_mode=resident),
            pl.BlockSpec((N_V, D_MODEL), fixed2, pipeline_mode=resident),
            pl.BlockSpec((LANE, D_MODEL), fixed2),
            pl.BlockSpec((LANE, 1), fixed2),
            pl.BlockSpec((N_QK // LANE, LANE, LANE), lambda i: (0, 0, 0), pipeline_mode=resident),
            pl.BlockSpec((1, N_QK), fixed2),
            pl.BlockSpec((1, N_QK), fixed2),
        ],
        out_specs=[
            pl.BlockSpec((tm, N_QK), lambda i: (i, 0)),
            pl.BlockSpec((N_V, tm), lambda i: (0, i)),
            pl.BlockSpec((tm, LANE), lambda i: (i, 0)),
        ],
        out_shape=[
            jax.ShapeDtypeStruct((s, N_QK), BF16),
            jax.ShapeDtypeStruct((N_V, s), BF16),
            jax.ShapeDtypeStruct((s, LANE), F32),
        ],
        scratch_shapes=[pltpu.VMEM((LANE, 1), F32)],
        compiler_params=pltpu.CompilerParams(
            dimension_semantics=("arbitrary",), vmem_limit_bytes=VMEM_LIMIT),
        name="proj",
    )(x, g, wqk, wv_t, wfc_t, bfc, gmat, eps, gain)


def _bias_lanes(split, ones_value, *, split_first):
    hi, mid, lo = split
    lane = lax.broadcasted_iota(jnp.int32, hi.shape, 1)
    s0, c0 = (0, 3) if split_first else (3, 0)
    const = jnp.where((lane >= c0) & (lane < c0 + 3), ones_value, 0.0).astype(BF16)
    return jnp.where(lane == s0, hi, jnp.where(lane == s0 + 1, mid, jnp.where(lane == s0 + 2, lo, const)))


def _kq_iotas(t):
    return (lax.broadcasted_iota(jnp.int32, (t, t), 0),
            lax.broadcasted_iota(jnp.int32, (t, t), 1))


def _fold_rows(x, op):
    while x.shape[0] > 8:
        half = x.shape[0] // 2
        x = op(x[:half], x[half:])
    return x


def _softmax_step(s, vt, m_scr, acc_scr):
    m_old = m_scr[...]
    m_new = jnp.maximum(m_old, jnp.max(_fold_rows(s, jnp.maximum), axis=0, keepdims=True))
    alpha = jnp.exp2(m_old - m_new)
    p = jnp.exp2(s - m_new).astype(BF16)
    vt_aug = jnp.concatenate([vt, jnp.ones((BF16_ROWS, vt.shape[1]), BF16)], axis=0)
    acc_scr[...] = alpha * acc_scr[...] + _dot(vt_aug, p)
    m_scr[...] = m_new


def _finish_head(o_t, gcol_ref, o_ref, rows):
    ms = jnp.mean(o_t * o_t, axis=0, keepdims=True)
    o_ref[rows, :] = (o_t * lax.rsqrt(ms + EPS) * gcol_ref[...]).T.astype(BF16)


def _pipelined_blocks(qi, issue_scores, consume):
    issue_scores(0, 0)

    def body(p, carry):
        issue_scores(2 * p + 1, 1)
        consume(2 * p, 0, False)
        issue_scores(2 * p + 2, 0)
        consume(2 * p + 1, 1, False)
        return carry

    lax.fori_loop(0, qi // 2, body, 0)

    @pl.when(qi % 2 == 0)
    def _():
        consume(qi, 0, True)

    @pl.when(qi % 2 == 1)
    def _():
        issue_scores(qi, 1)
        consume(qi - 1, 0, False)
        consume(qi, 1, True)


def _fox_kernel(q_ref, k_ref, vt_ref, d_ref, g_ref, o_ref, kaug_scr, s_scr, m_scr, acc_scr, *, t, nblk):
    h = pl.program_id(0)
    pick = (lax.broadcasted_iota(jnp.int32, (LANE, LANE), 0) == h).astype(BF16)

    def head_bias(rows):
        return tuple(_dot(p, pick).astype(BF16) for p in _split3_bf16(d_ref[rows, :]))

    def build(b, carry):
        rows = pl.ds(pl.multiple_of(b * t, t), t)
        kaug_scr[rows, :LANE] = k_ref[rows, :]
        kaug_scr[rows, LANE:] = _bias_lanes(head_bias(rows), 1.0, split_first=True)
        return carry

    lax.fori_loop(0, nblk, build, 0)

    def query_block(qi, carry):
        qrows = pl.ds(pl.multiple_of(qi * t, t), t)
        q_aug = jnp.concatenate(
            [q_ref[qrows, :], _bias_lanes(head_bias(qrows), -1.0, split_first=False)], axis=1)
        m_scr[...] = jnp.full_like(m_scr, -jnp.inf)
        acc_scr[...] = jnp.zeros_like(acc_scr)

        def issue_scores(j, slot):
            s_scr[slot] = _dot_nt(kaug_scr[pl.ds(pl.multiple_of(j * t, t), t), :], q_aug)

        def consume(j, slot, masked):
            s = s_scr[slot]
            if masked:
                kpos, qpos = _kq_iotas(t)
                s = jnp.where(kpos <= qpos, s, -jnp.inf)
            _softmax_step(s, vt_ref[:, pl.ds(pl.multiple_of(j * t, t), t)], m_scr, acc_scr)

        _pipelined_blocks(qi, issue_scores, consume)
        acc = acc_scr[...]
        _finish_head(acc[:HEAD_DIM] / acc[HEAD_DIM:HEAD_DIM + 1], g_ref, o_ref, qrows)
        return carry

    lax.fori_loop(0, nblk, query_block, 0)


def _diff_kernel(slope_ref, lam_ref, q_ref, k_ref, vt_ref, g_ref, o_ref,
                 kaug_scr, s_scr, m1_scr, a1_scr, m2_scr, a2_scr, *, t, nblk):
    h = pl.program_id(0)
    slope = slope_ref[h]
    lam = lam_ref[0]

    def pos_split(block, sign):
        pos = (block * t + lax.broadcasted_iota(jnp.int32, (t, LANE), 0)).astype(F32)
        return _split3_bf16(sign * slope * pos)

    def build(b, carry):
        rows = pl.ds(pl.multiple_of(b * t, t), t)
        kaug_scr[rows, :LANE] = k_ref[rows, :]
        kaug_scr[rows, LANE:] = _bias_lanes(pos_split(b, 1.0), 1.0, split_first=True)
        return carry

    lax.fori_loop(0, nblk, build, 0)

    def query_block(qi, carry):
        qrows = pl.ds(pl.multiple_of(qi * t, t), t)
        q = q_ref[qrows, :]
        lane = lax.broadcasted_iota(jnp.int32, q.shape, 1)
        zero = jnp.zeros_like(q)
        q_bias = _bias_lanes(pos_split(qi, -1.0), 1.0, split_first=False)
        q1_aug = jnp.concatenate([jnp.where(lane < DIFF_HALF, q, zero), q_bias], axis=1)
        q2_aug = jnp.concatenate([jnp.where(lane >= DIFF_HALF, q, zero), q_bias], axis=1)
        for m_scr, a_scr in ((m1_scr, a1_scr), (m2_scr, a2_scr)):
            m_scr[...] = jnp.full_like(m_scr, -jnp.inf)
            a_scr[...] = jnp.zeros_like(a_scr)

        def issue_scores(j, slot):
            kb = kaug_scr[pl.ds(pl.multiple_of(j * t, t), t), :]
            s_scr[slot, 0] = _dot_nt(kb, q1_aug)
            s_scr[slot, 1] = _dot_nt(kb, q2_aug)

        def consume(j, slot, masked):
            vt = vt_ref[:, pl.ds(pl.multiple_of(j * t, t), t)]
            for idx, (m_scr, a_scr) in enumerate(((m1_scr, a1_scr), (m2_scr, a2_scr))):
                s = s_scr[slot, idx]
                if masked:
                    kpos, qpos = _kq_iotas(t)
                    ahead = (kpos - qpos).astype(F32)
                    s = s - jnp.where(ahead > 0.0, (2.0 * slope) * ahead, 0.0)
                    s = jnp.where((kpos // CHUNK) <= (qpos // CHUNK), s, -jnp.inf)
                _softmax_step(s, vt, m_scr, a_scr)

        _pipelined_blocks(qi, issue_scores, consume)
        a1 = a1_scr[...]
        a2 = a2_scr[...]
        o_t = (a1[:HEAD_DIM] / a1[HEAD_DIM:HEAD_DIM + 1]
               - lam * (a2[:HEAD_DIM] / a2[HEAD_DIM:HEAD_DIM + 1]))
        _finish_head(o_t, g_ref, o_ref, qrows)
        return carry

    lax.fori_loop(0, nblk, query_block, 0)


def _sb_kernel(q_ref, k_ref, vt_ref, g_ref, o_ref, s_scr, run_scr, acc_scr, *, t, tc, nblk):
    ur = lax.broadcasted_iota(jnp.int32, (tc, tc), 0)
    uc = lax.broadcasted_iota(jnp.int32, (tc, tc), 1)
    later = jnp.concatenate([(uc > ur).astype(BF16), jnp.ones((BF16_ROWS, tc), BF16)], axis=0)
    nchunk = t // tc

    def query_block(qi, carry):
        qrows = pl.ds(pl.multiple_of(qi * t, t), t)
        q = q_ref[qrows, :]
        run_scr[...] = jnp.zeros_like(run_scr)
        acc_scr[...] = jnp.zeros_like(acc_scr)

        def issue_scores(j, slot):
            s_scr[slot] = _dot_nt(k_ref[pl.ds(pl.multiple_of(j * t, t), t), :], q)

        def consume(j, slot, masked):
            z = s_scr[slot]
            log_beta = jnp.minimum(z, 0.0) - jnp.log2(1.0 + jnp.exp2(-jnp.abs(z)))
            log_keep = log_beta - z
            if masked:
                kpos, qpos = _kq_iotas(t)
                strict = kpos < qpos
                log_keep = jnp.where(strict, log_keep, 0.0)
            run = run_scr[...]
            acc = acc_scr[...]
            for ci in range(nchunk - 1, -1, -1):
                sl = slice(ci * tc, (ci + 1) * tc)
                sums = _dot(later, log_keep[sl].astype(BF16))
                a = jnp.exp2(log_beta[sl] + sums[:tc])
                if masked:
                    a = jnp.where(strict[sl], a, 0.0)
                cols = pl.ds(pl.multiple_of(j * t + ci * tc, tc), tc)
                acc = acc + jnp.exp2(run) * _dot(vt_ref[:, cols], a.astype(BF16))
                run = run + sums[tc:tc + 1]
            acc_scr[...] = acc
            run_scr[...] = run

        issue_scores(qi, 0)
        issue_scores(jnp.maximum(qi - 1, 0), 1)
        consume(qi, 0, True)

        def rest(p, carry):
            issue_scores(jnp.maximum(qi - 2 * p - 2, 0), 0)
            consume(qi - 2 * p - 1, 1, False)
            issue_scores(jnp.maximum(qi - 2 * p - 3, 0), 1)
            consume(qi - 2 * p - 2, 0, False)
            return carry

        lax.fori_loop(0, qi // 2, rest, 0)

        @pl.when(qi % 2 == 1)
        def _():
            consume(0, 1, False)

        _finish_head(acc_scr[...], g_ref, o_ref, qrows)
        return carry

    lax.fori_loop(0, nblk, query_block, 0)


def _attn_specs(s, qc, kc, vc):
    return [
        pl.BlockSpec((s, LANE), lambda h: (0, qc + h)),
        pl.BlockSpec((s, LANE), lambda h: (0, kc + h)),
        pl.BlockSpec((LANE, s), lambda h: (vc + h, 0)),
    ]


_ATTN_PARAMS = pltpu.CompilerParams(
    dimension_semantics=("arbitrary",), vmem_limit_bytes=VMEM_LIMIT)
_GCOL_SPEC = pl.BlockSpec((LANE, 1), lambda h: (0, 0))


def _fox(qk, vt, dtok, gcol, *, t):
    s = qk.shape[0]
    return pl.pallas_call(
        functools.partial(_fox_kernel, t=t, nblk=s // t),
        grid=(H_FOX,),
        in_specs=_attn_specs(s, QC, KC, VC) + [
            pl.BlockSpec((s, LANE), lambda h: (0, 0)),
            _GCOL_SPEC,
        ],
        out_specs=pl.BlockSpec((s, LANE), lambda h: (0, h)),
        out_shape=jax.ShapeDtypeStruct((s, H_FOX * LANE), BF16),
        scratch_shapes=[
            pltpu.VMEM((s, 2 * LANE), BF16),
            pltpu.VMEM((2, t, t), F32),
            pltpu.VMEM((1, t), F32),
            pltpu.VMEM((HEAD_DIM + BF16_ROWS, t), F32),
        ],
        compiler_params=_ATTN_PARAMS,
        name="fox_attn",
    )(qk, qk, vt, dtok, gcol)


def _diff(qk, vt, slopes, lam, gcol, *, t):
    s = qk.shape[0]
    row = pltpu.VMEM((1, t), F32)
    acc = pltpu.VMEM((HEAD_DIM + BF16_ROWS, t), F32)
    smem = pl.BlockSpec(memory_space=pltpu.SMEM)
    return pl.pallas_call(
        functools.partial(_diff_kernel, t=t, nblk=s // t),
        grid=(H_DIFF,),
        in_specs=[smem, smem] + _attn_specs(s, QA, KA, VA) + [_GCOL_SPEC],
        out_specs=pl.BlockSpec((s, LANE), lambda h: (0, h)),
        out_shape=jax.ShapeDtypeStruct((s, H_DIFF * LANE), BF16),
        scratch_shapes=[
            pltpu.VMEM((s, 2 * LANE), BF16),
            pltpu.VMEM((2, 2, t, t), F32),
            row, acc, row, acc,
        ],
        compiler_params=_ATTN_PARAMS,
        name="diff_attn",
    )(slopes, lam, qk, qk, vt, gcol)


def _sb(qk, vt, gcol, *, t, tc):
    s = qk.shape[0]
    return pl.pallas_call(
        functools.partial(_sb_kernel, t=t, tc=tc, nblk=s // t),
        grid=(H_SB,),
        in_specs=_attn_specs(s, QB, KB, VB) + [_GCOL_SPEC],
        out_specs=pl.BlockSpec((s, LANE), lambda h: (0, h)),
        out_shape=jax.ShapeDtypeStruct((s, H_SB * LANE), BF16),
        scratch_shapes=[
            pltpu.VMEM((2, t, t), F32),
            pltpu.VMEM((1, t), F32),
            pltpu.VMEM((HEAD_DIM, t), F32),
        ],
        compiler_params=_ATTN_PARAMS,
        name="sb_attn",
    )(qk, qk, vt, gcol)


def _route(logits_t, bias):
    score = [jax.nn.sigmoid(logits_t[e:e + 1, :]) for e in range(N_EXPERTS)]
    sel = [score[e] + bias[e:e + 1, :] for e in range(N_EXPERTS)]
    gscore = []
    for g in range(N_GROUPS):
        mem = sel[g * EXPERTS_PER_GROUP:(g + 1) * EXPERTS_PER_GROUP]
        best = None
        for a in range(EXPERTS_PER_GROUP):
            for b in range(a + 1, EXPERTS_PER_GROUP):
                pair = mem[a] + mem[b]
                best = pair if best is None else jnp.maximum(best, pair)
        gscore.append(best)
    gmax = functools.reduce(jnp.maximum, gscore)
    taken = None
    in_group = []
    for g in range(N_GROUPS):
        hit = gscore[g] == gmax
        if taken is not None:
            hit = hit & ~taken
        taken = hit if taken is None else taken | hit
        in_group.append(hit)
    masked = [jnp.where(in_group[e // EXPERTS_PER_GROUP], sel[e], -jnp.inf) for e in range(N_EXPERTS)]

    def first_max(vals):
        top = functools.reduce(jnp.maximum, vals)
        seen = None
        picks = []
        for v in vals:
            hit = v == top
            if seen is not None:
                hit = hit & ~seen
            seen = hit if seen is None else seen | hit
            picks.append(hit)
        return picks

    pick1 = first_max(masked)
    rest = [jnp.where(pick1[e], -jnp.inf, masked[e]) for e in range(N_EXPERTS)]
    pick2 = first_max(rest)
    chosen = [pick1[e] | pick2[e] for e in range(N_EXPERTS)]
    gate = [jnp.where(chosen[e], score[e], 0.0) for e in range(N_EXPERTS)]
    total = functools.reduce(lambda a, b: a + b, gate)
    return jnp.concatenate([gt / total for gt in gate], axis=0)


def _outproj_kernel(x_ref, oa_ref, ob_ref, oc_ref, wa_ref, wb_ref, wc_ref, g_ref,
                    wrh_ref, wrl_ref, rb_ref, h_ref, hn_ref, comb_ref, *, tm):
    h = x_ref[...] + _dot(oa_ref[...], wa_ref[...]) + _dot(ob_ref[...], wb_ref[...]) \
        + _dot(oc_ref[...], wc_ref[...])
    h_ref[...] = h
    ms = jnp.mean(h * h, axis=-1, keepdims=True)
    hn = h * lax.rsqrt(ms + EPS) * g_ref[...]
    hi, lo = _split_bf16(hn)
    hn_ref[...] = hi
    wrh = wrh_ref[...]
    logits_t = _dot_nt(wrh, hi) + _dot_nt(wrh, lo) + _dot_nt(wrl_ref[...], hi)
    comb = _route(logits_t, rb_ref[...])
    pad = jnp.zeros((LANE - N_EXPERTS, tm), F32)
    comb_ref[...] = jnp.concatenate([comb, pad], axis=0).T


def _outproj(x, oa, ob, oc, wa, wb, wc, g, wrh, wrl, rb, *, tm):
    s = x.shape[0]
    row = lambda i: (i, 0)
    fixed = lambda i: (0, 0)
    return pl.pallas_call(
        functools.partial(_outproj_kernel, tm=tm),
        grid=(s // tm,),
        in_specs=[
            pl.BlockSpec((tm, D_MODEL), row),
            pl.BlockSpec((tm, H_DIFF * LANE), row),
            pl.BlockSpec((tm, H_SB * LANE), row),
            pl.BlockSpec((tm, H_FOX * LANE), row),
            pl.BlockSpec((H_DIFF * LANE, D_MODEL), fixed, pipeline_mode=pl.Buffered(1)),
            pl.BlockSpec((H_SB * LANE, D_MODEL), fixed, pipeline_mode=pl.Buffered(1)),
            pl.BlockSpec((H_FOX * LANE, D_MODEL), fixed, pipeline_mode=pl.Buffered(1)),
            pl.BlockSpec((1, D_MODEL), fixed),
            pl.BlockSpec((N_EXPERTS, D_MODEL), fixed),
            pl.BlockSpec((N_EXPERTS, D_MODEL), fixed),
            pl.BlockSpec((N_EXPERTS, 1), fixed),
        ],
        out_specs=[
            pl.BlockSpec((tm, D_MODEL), row),
            pl.BlockSpec((tm, D_MODEL), row),
            pl.BlockSpec((tm, LANE), row),
        ],
        out_shape=[
            jax.ShapeDtypeStruct((s, D_MODEL), F32),
            jax.ShapeDtypeStruct((s, D_MODEL), BF16),
            jax.ShapeDtypeStruct((s, LANE), F32),
        ],
        compiler_params=pltpu.CompilerParams(
            dimension_semantics=("arbitrary",), vmem_limit_bytes=VMEM_LIMIT),
        name="outproj_router",
    )(x, oa, ob, oc, wa, wb, wc, g, wrh, wrl, rb)


def _moe_kernel(h_ref, hn_ref, comb_ref, wg_ref, wu_ref, wd_ref, o_ref):
    e = pl.program_id(1)

    @pl.when(e == 0)
    def _():
        o_ref[...] = h_ref[...]

    hn = hn_ref[...]
    comb = comb_ref[...]
    lane = lax.broadcasted_iota(jnp.int32, comb.shape, 1)
    weight = jnp.sum(jnp.where(lane == e, comb, 0.0), axis=-1, keepdims=True)
    gate = _dot(hn, wg_ref[...])
    up = _dot(hn, wu_ref[...])
    act = gate * jax.nn.sigmoid(gate) * up * weight
    o_ref[...] += _dot(act.astype(BF16), wd_ref[...])


def _moe(h, hn, comb, wg, wu, wd, *, tm):
    s = h.shape[0]
    row = lambda i, e: (i, 0)
    return pl.pallas_call(
        _moe_kernel,
        grid=(s // tm, N_EXPERTS),
        in_specs=[
            pl.BlockSpec((tm, D_MODEL), row),
            pl.BlockSpec((tm, D_MODEL), row),
            pl.BlockSpec((tm, LANE), row),
            pl.BlockSpec((None, D_MODEL, D_EXPERT), lambda i, e: (e, 0, 0)),
            pl.BlockSpec((None, D_MODEL, D_EXPERT), lambda i, e: (e, 0, 0)),
            pl.BlockSpec((None, D_EXPERT, D_MODEL), lambda i, e: (e, 0, 0)),
        ],
        out_specs=pl.BlockSpec((tm, D_MODEL), row),
        out_shape=jax.ShapeDtypeStruct((s, D_MODEL), F32),
        compiler_params=pltpu.CompilerParams(
            dimension_semantics=("arbitrary", "arbitrary"), vmem_limit_bytes=VMEM_LIMIT),
        name="moe_experts",
    )(h, hn, comb, wg, wu, wd)


def _proj_epilogue_tables(qnorm_diff, knorm_diff, qnorm_fox, knorm_fox):
    ones = jnp.ones((HEAD_DIM,), F32)
    normed = jnp.full((HEAD_DIM,), EPS, F32)
    plain = ones
    half = jnp.arange(HEAD_DIM) // DIFF_HALF
    g_diff = (half[:, None] == half[None, :]).astype(F32) / DIFF_HALF
    g_full = jnp.full((HEAD_DIM, HEAD_DIM), 1.0 / HEAD_DIM, F32)
    g_none = jnp.zeros((HEAD_DIM, HEAD_DIM), F32)
    diff_scale = DIFF_HALF ** -0.5 * LOG2E
    full_scale = HEAD_DIM ** -0.5 * LOG2E
    groups = [
        (H_DIFF, g_diff, normed, jnp.tile(qnorm_diff, 2) * diff_scale),
        (H_DIFF, g_diff, normed, jnp.tile(knorm_diff, 2)),
        (H_SB, g_none, plain, ones * full_scale),
        (H_SB, g_none, plain, ones),
        (H_FOX, g_full, normed, qnorm_fox * full_scale),
        (H_FOX, g_full, normed, knorm_fox),
    ]
    gmat = jnp.concatenate([jnp.broadcast_to(gm, (n, HEAD_DIM, HEAD_DIM)) for n, gm, _, _ in groups])
    eps = jnp.concatenate([jnp.tile(ep, n) for n, _, ep, _ in groups])[None, :]
    gain = jnp.concatenate([jnp.tile(gn, n) for n, _, _, gn in groups])[None, :]
    return gmat.astype(BF16), eps, gain


def _split_in_weights(w):
    wa, wb, wc = H_DIFF * HEAD_DIM, H_SB * HEAD_DIM, H_FOX * HEAD_DIM
    pieces = []
    off = 0
    for width in (wa, wa, wa, wb, wb, wb, wc, wc, wc):
        pieces.append(w[:, off:off + width])
        off += width
    qa, ka, va, qb, kb, vb, qc, kc, vc = pieces
    return (jnp.concatenate([qa, ka, qb, kb, qc, kc], axis=1),
            jnp.concatenate([va, vb, vc], axis=1), w[:, off:])


def _tiles(s):
    return dict(proj_tm=min(256, s), proj_tn=512, attn_t=min(512, s), sb_tc=256,
                out_tm=min(512, s), moe_tm=min(512, s))


def kernel(x, attn_norm, w_in, b_forget, qnorm_diff, knorm_diff, lam_q1, lam_k1, lam_q2, lam_k2,
           subln_diff, onorm_sb, qnorm_fox, knorm_fox, onorm_fox, w_out, ffn_norm,
           w_router, router_bias, w_gate, w_up, w_down):
    b, s, d = x.shape
    assert b == 1 and d == D_MODEL
    depth = w_in.shape[0]
    tl = _tiles(s)
    t = tl["attn_t"]
    slopes = jnp.exp2(-8.0 * jnp.arange(1, H_DIFF + 1, dtype=F32) / H_DIFF) * LOG2E
    wr_t = w_router.T.astype(F32)
    wrh = wr_t.astype(BF16)
    wrl = (wr_t - wrh.astype(F32)).astype(BF16)
    rb = router_bias.astype(F32)[:, None]
    xs = x[0]
    w_in_b = lax.optimization_barrier(w_in.astype(BF16))
    for l in range(depth):
        w_qk, w_v, w_fc = _split_in_weights(w_in_b[l])
        wfc_t = jnp.zeros((LANE, D_MODEL), BF16).at[:H_FOX].set(w_fc.T)
        bfc = jnp.zeros((LANE, 1), F32).at[:H_FOX, 0].set(b_forget[l])
        gmat, eps, gain = _proj_epilogue_tables(qnorm_diff[l], knorm_diff[l], qnorm_fox[l], knorm_fox[l])
        qk, vt, dtok = _proj(xs, attn_norm[l][None, :], w_qk, w_v.T, wfc_t, bfc, gmat, eps, gain,
                             tm=tl["proj_tm"], tn=tl["proj_tn"])

        lam_init = 0.8 - 0.6 * math.exp(-0.3 * l)
        lam = (jnp.exp(jnp.sum(lam_q1[l] * lam_k1[l])) - jnp.exp(jnp.sum(lam_q2[l] * lam_k2[l]))
               + lam_init).reshape(1).astype(F32)
        oa = _diff(qk, vt, slopes, lam, (subln_diff[l] * (1.0 - lam_init))[:, None], t=t)
        ob = _sb(qk, vt, onorm_sb[l][:, None], t=t, tc=min(tl["sb_tc"], t))
        oc = _fox(qk, vt, dtok, onorm_fox[l][:, None], t=t)

        wo = w_out[l].astype(BF16)
        na, nb = H_DIFF * LANE, (H_DIFF + H_SB) * LANE
        h, hn, comb = _outproj(xs, oa, ob, oc, wo[:na], wo[na:nb], wo[nb:], ffn_norm[l][None, :],
                               wrh, wrl, rb, tm=tl["out_tm"])
        xs = _moe(h, hn, comb, w_gate[l].astype(BF16), w_up[l].astype(BF16), w_down[l].astype(BF16),
                  tm=tl["moe_tm"])
    return xs[None]
```

```python
import functools
import math

import jax
import jax.numpy as jnp
from jax import lax
from jax.experimental import pallas as pl
from jax.experimental.pallas import tpu as pltpu

F32 = jnp.float32
BF16 = jnp.bfloat16

D_MODEL = 2048
HEAD_DIM = 128
LANE = 128
BF16_ROWS = 16
N_HEADS = 16
H_SB = N_HEADS // 3
H_FOX = N_HEADS // 3
H_DIFF = N_HEADS - H_SB - H_FOX
DIFF_HALF = HEAD_DIM // 2
CHUNK = 64
N_EXPERTS = 16
N_GROUPS = 4
EXPERTS_PER_GROUP = N_EXPERTS // N_GROUPS
D_EXPERT = D_MODEL // 4
EPS = 1e-6
LOG2E = 1.4426950408889634

QA, KA = 0, H_DIFF
QB, KB = 2 * H_DIFF, 2 * H_DIFF + H_SB
QC, KC = 2 * H_DIFF + 2 * H_SB, 2 * H_DIFF + 2 * H_SB + H_FOX
N_QK = 2 * N_HEADS * HEAD_DIM
VA, VB, VC = 0, H_DIFF, H_DIFF + H_SB
N_V = N_HEADS * HEAD_DIM

VMEM_LIMIT = 48 * 1024 * 1024

_NT = (((1,), (1,)), ((), ()))


def _dot(a, b):
    return jnp.dot(a, b, preferred_element_type=F32)


def _dot_nt(a, b):
    return lax.dot_general(a, b, _NT, preferred_element_type=F32)


def _log_sigmoid(x):
    return jnp.minimum(x, 0.0) - jnp.log1p(jnp.exp(-jnp.abs(x)))


def _split_bf16(x):
    hi = x.astype(BF16)
    lo = (x - hi.astype(F32)).astype(BF16)
    return hi, lo


def _split3_bf16(x):
    hi = x.astype(BF16)
    r = x - hi.astype(F32)
    mid = r.astype(BF16)
    lo = (r - mid.astype(F32)).astype(BF16)
    return hi, mid, lo


def _proj_kernel(x_ref, g_ref, wqk_ref, wvt_ref, wfc_ref, bfc_ref, gmat_ref, eps_ref, gain_ref,
                 qk_ref, vt_ref, dtok_ref, carry_scr, *, tm, tn):
    x = x_ref[...]
    ms = jnp.mean(x * x, axis=-1, keepdims=True)
    xn = (x * lax.rsqrt(ms + EPS) * g_ref[...]).astype(BF16)

    @pl.when(pl.program_id(0) == 0)
    def _():
        carry_scr[...] = jnp.zeros_like(carry_scr)

    lf = _log_sigmoid(_dot_nt(wfc_ref[...], xn) + bfc_ref[...])
    hi, lo = _split_bf16(lf)
    r = lax.broadcasted_iota(jnp.int32, (tm, tm), 0)
    c = lax.broadcasted_iota(jnp.int32, (tm, tm), 1)
    tri = (r <= c).astype(BF16)
    cum = _dot(hi, tri) + _dot(lo, tri) + carry_scr[...]
    carry_scr[...] = cum[:, tm - 1:tm]
    dtok_ref[...] = (cum * LOG2E).T

    def project(j):
        return _dot(xn, wqk_ref[:, pl.ds(pl.multiple_of(j * tn, tn), tn)])

    def epilogue(j, acc):
        for c in range(tn // LANE):
            cols = pl.ds(pl.multiple_of(j * tn + c * LANE, LANE), LANE)
            y = acc[:, c * LANE:(c + 1) * LANE]
            ss = _dot((y * y).astype(BF16), gmat_ref[j * (tn // LANE) + c])
            qk_ref[:, cols] = (y * lax.rsqrt(ss + eps_ref[:, cols]) * gain_ref[:, cols]).astype(BF16)

    def qk_block(j, acc):
        nxt = project(j + 1)
        epilogue(j, acc)
        return nxt

    last = N_QK // tn - 1
    epilogue(last, lax.fori_loop(0, last, qk_block, project(0)))

    def v_block(j, carry):
        rows = pl.ds(pl.multiple_of(j * tn, tn), tn)
        vt_ref[rows, :] = _dot_nt(wvt_ref[rows, :], xn).astype(BF16)
        return carry

    lax.fori_loop(0, N_V // tn, v_block, 0)


def _proj(x, g, wqk, wv_t, wfc_t, bfc, gmat, eps, gain, *, tm, tn):
    s = x.shape[0]
    fixed2 = lambda i: (0, 0)
    resident = pl.Buffered(1)
    return pl.pallas_call(
        functools.partial(_proj_kernel, tm=tm, tn=tn),
        grid=(s // tm,),
        in_specs=[
            pl.BlockSpec((tm, D_MODEL), lambda i: (i, 0)),
            pl.BlockSpec((1, D_MODEL), fixed2),
            pl.BlockSpec((D_MODEL, N_QK), fixed2, pipeline_mode=resident),
            pl.BlockSpec((N_V, D_MODEL), fixed2, pipeline_mode=resident),
            pl.BlockSpec((LANE, D_MODEL), fixed2),
            pl.BlockSpec((LANE, 1), fixed2),
            pl.BlockSpec((N_QK // LANE, LANE, LANE), lambda i: (0, 0, 0), pipeline_mode=resident),
            pl.BlockSpec((1, N_QK), fixed2),
            pl.BlockSpec((1, N_QK), fixed2),
        ],
        out_specs=[
            pl.BlockSpec((tm, N_QK), lambda i: (i, 0)),
            pl.BlockSpec((N_V, tm), lambda i: (0, i)),
            pl.BlockSpec((tm, LANE), lambda i: (i, 0)),
        ],
        out_shape=[
            jax.ShapeDtypeStruct((s, N_QK), BF16),
            jax.ShapeDtypeStruct((N_V, s), BF16),
            jax.ShapeDtypeStruct((s, LANE), F32),
        ],
        scratch_shapes=[pltpu.VMEM((LANE, 1), F32)],
        compiler_params=pltpu.CompilerParams(
            dimension_semantics=("arbitrary",), vmem_limit_bytes=VMEM_LIMIT),
        name="proj",
    )(x, g, wqk, wv_t, wfc_t, bfc, gmat, eps, gain)


def _bias_lanes(split, ones_value, *, split_first):
    hi, mid, lo = split
    lane = lax.broadcasted_iota(jnp.int32, hi.shape, 1)
    s0, c0 = (0, 3) if split_first else (3, 0)
    const = jnp.where((lane >= c0) & (lane < c0 + 3), ones_value, 0.0).astype(BF16)
    return jnp.where(lane == s0, hi, jnp.where(lane == s0 + 1, mid, jnp.where(lane == s0 + 2, lo, const)))


def _kq_iotas(t):
    return (lax.broadcasted_iota(jnp.int32, (t, t), 0),
            lax.broadcasted_iota(jnp.int32, (t, t), 1))


def _fold_rows(x, op):
    while x.shape[0] > 8:
        half = x.shape[0] // 2
        x = op(x[:half], x[half:])
    return x


def _softmax_step(s, vt, m_scr, acc_scr):
    m_old = m_scr[...]
    m_new = jnp.maximum(m_old, jnp.max(_fold_rows(s, jnp.maximum), axis=0, keepdims=True))
    alpha = jnp.exp2(m_old - m_new)
    p = jnp.exp2((s - m_new).astype(BF16))
    vt_aug = jnp.concatenate([vt, jnp.ones((BF16_ROWS, vt.shape[1]), BF16)], axis=0)
    acc_scr[...] = alpha * acc_scr[...] + _dot(vt_aug, p)
    m_scr[...] = m_new


def _finish_head(o_t, gcol_ref, o_ref, rows):
    ms = jnp.mean(o_t * o_t, axis=0, keepdims=True)
    o_ref[rows, :] = (o_t * lax.rsqrt(ms + EPS) * gcol_ref[...]).T.astype(BF16)


def _pipelined_blocks(qi, issue_scores, consume):
    issue_scores(0, 0)

    def body(p, carry):
        issue_scores(2 * p + 1, 1)
        consume(2 * p, 0, False)
        issue_scores(2 * p + 2, 0)
        consume(2 * p + 1, 1, False)
        return carry

    lax.fori_loop(0, qi // 2, body, 0)

    @pl.when(qi % 2 == 0)
    def _():
        consume(qi, 0, True)

    @pl.when(qi % 2 == 1)
    def _():
        issue_scores(qi, 1)
        consume(qi - 1, 0, False)
        consume(qi, 1, True)


def _fox_kernel(q_ref, k_ref, vt_ref, d_ref, g_ref, o_ref, kaug_scr, s_scr, m_scr, acc_scr, *, t, nblk):
    h = pl.program_id(0)
    pick = (lax.broadcasted_iota(jnp.int32, (LANE, LANE), 0) == h).astype(BF16)

    def head_bias(rows):
        return tuple(_dot(p, pick).astype(BF16) for p in _split3_bf16(d_ref[rows, :]))

    def build(b, carry):
        rows = pl.ds(pl.multiple_of(b * t, t), t)
        kaug_scr[rows, :LANE] = k_ref[rows, :]
        kaug_scr[rows, LANE:] = _bias_lanes(head_bias(rows), 1.0, split_first=True)
        return carry

    lax.fori_loop(0, nblk, build, 0)

    def query_block(qi, carry):
        qrows = pl.ds(pl.multiple_of(qi * t, t), t)
        q_aug = jnp.concatenate(
            [q_ref[qrows, :], _bias_lanes(head_bias(qrows), -1.0, split_first=False)], axis=1)
        m_scr[...] = jnp.full_like(m_scr, -jnp.inf)
        acc_scr[...] = jnp.zeros_like(acc_scr)

        def issue_scores(j, slot):
            s_scr[slot] = _dot_nt(kaug_scr[pl.ds(pl.multiple_of(j * t, t), t), :], q_aug)

        def consume(j, slot, masked):
            s = s_scr[slot]
            if masked:
                kpos, qpos = _kq_iotas(t)
                s = jnp.where(kpos <= qpos, s, -jnp.inf)
            _softmax_step(s, vt_ref[:, pl.ds(pl.multiple_of(j * t, t), t)], m_scr, acc_scr)

        _pipelined_blocks(qi, issue_scores, consume)
        acc = acc_scr[...]
        _finish_head(acc[:HEAD_DIM] / acc[HEAD_DIM:HEAD_DIM + 1], g_ref, o_ref, qrows)
        return carry

    lax.fori_loop(0, nblk, query_block, 0)


def _diff_kernel(slope_ref, lam_ref, q_ref, k_ref, vt_ref, g_ref, o_ref,
                 kaug_scr, s_scr, m1_scr, a1_scr, m2_scr, a2_scr, *, t, nblk):
    h = pl.program_id(0)
    slope = slope_ref[h]
    lam = lam_ref[0]

    def pos_split(block, sign):
        pos = (block * t + lax.broadcasted_iota(jnp.int32, (t, LANE), 0)).astype(F32)
        return _split3_bf16(sign * slope * pos)

    def build(b, carry):
        rows = pl.ds(pl.multiple_of(b * t, t), t)
        kaug_scr[rows, :LANE] = k_ref[rows, :]
        kaug_scr[rows, LANE:] = _bias_lanes(pos_split(b, 1.0), 1.0, split_first=True)
        return carry

    lax.fori_loop(0, nblk, build, 0)

    def query_block(qi, carry):
        qrows = pl.ds(pl.multiple_of(qi * t, t), t)
        q = q_ref[qrows, :]
        lane = lax.broadcasted_iota(jnp.int32, q.shape, 1)
        zero = jnp.zeros_like(q)
        q_bias = _bias_lanes(pos_split(qi, -1.0), 1.0, split_first=False)
        q1_aug = jnp.concatenate([jnp.where(lane < DIFF_HALF, q, zero), q_bias], axis=1)
        q2_aug = jnp.concatenate([jnp.where(lane >= DIFF_HALF, q, zero), q_bias], axis=1)
        for m_scr, a_scr in ((m1_scr, a1_scr), (m2_scr, a2_scr)):
            m_scr[...] = jnp.full_like(m_scr, -jnp.inf)
            a_scr[...] = jnp.zeros_like(a_scr)

        def issue_scores(j, slot):
            kb = kaug_scr[pl.ds(pl.multiple_of(j * t, t), t), :]
            s_scr[slot, 0] = _dot_nt(kb, q1_aug)
            s_scr[slot, 1] = _dot_nt(kb, q2_aug)

        def consume(j, slot, masked):
            vt = vt_ref[:, pl.ds(pl.multiple_of(j * t, t), t)]
            for idx, (m_scr, a_scr) in enumerate(((m1_scr, a1_scr), (m2_scr, a2_scr))):
                s = s_scr[slot, idx]
                if masked:
                    kpos, qpos = _kq_iotas(t)
                    ahead = (kpos - qpos).astype(F32)
                    s = s - jnp.where(ahead > 0.0, (2.0 * slope) * ahead, 0.0)
                    s = jnp.where((kpos // CHUNK) <= (qpos // CHUNK), s, -jnp.inf)
                _softmax_step(s, vt, m_scr, a_scr)

        _pipelined_blocks(qi, issue_scores, consume)
        a1 = a1_scr[...]
        a2 = a2_scr[...]
        o_t = (a1[:HEAD_DIM] / a1[HEAD_DIM:HEAD_DIM + 1]
               - lam * (a2[:HEAD_DIM] / a2[HEAD_DIM:HEAD_DIM + 1]))
        _finish_head(o_t, g_ref, o_ref, qrows)
        return carry

    lax.fori_loop(0, nblk, query_block, 0)


def _sb_kernel(q_ref, k_ref, vt_ref, g_ref, o_ref, s_scr, run_scr, acc_scr, *, t, tc, nblk):
    ur = lax.broadcasted_iota(jnp.int32, (tc, tc), 0)
    uc = lax.broadcasted_iota(jnp.int32, (tc, tc), 1)
    later = jnp.concatenate([(uc > ur).astype(BF16), jnp.ones((BF16_ROWS, tc), BF16)], axis=0)
    nchunk = t // tc

    def query_block(qi, carry):
        qrows = pl.ds(pl.multiple_of(qi * t, t), t)
        q = q_ref[qrows, :]
        run_scr[...] = jnp.zeros_like(run_scr)
        acc_scr[...] = jnp.zeros_like(acc_scr)

        def issue_scores(j, slot):
            s_scr[slot] = _dot_nt(k_ref[pl.ds(pl.multiple_of(j * t, t), t), :], q)

        def consume(j, slot, masked):
            z = s_scr[slot]
            log_beta = jnp.minimum(z, 0.0) - jnp.log2(1.0 + jnp.exp2(-jnp.abs(z)))
            log_keep = log_beta - z
            if masked:
                kpos, qpos = _kq_iotas(t)
                strict = kpos < qpos
                log_keep = jnp.where(strict, log_keep, 0.0)
            run = run_scr[...]
            acc = acc_scr[...]
            for ci in range(nchunk - 1, -1, -1):
                sl = slice(ci * tc, (ci + 1) * tc)
                sums = _dot(later, log_keep[sl].astype(BF16))
                a = jnp.exp2((log_beta[sl] + sums[:tc]).astype(BF16))
                if masked:
                    a = jnp.where(strict[sl], a, jnp.zeros_like(a))
                cols = pl.ds(pl.multiple_of(j * t + ci * tc, tc), tc)
                acc = acc + jnp.exp2(run) * _dot(vt_ref[:, cols], a)
                run = run + sums[tc:tc + 1]
            acc_scr[...] = acc
            run_scr[...] = run

        issue_scores(qi, 0)
        issue_scores(jnp.maximum(qi - 1, 0), 1)
        consume(qi, 0, True)

        def rest(p, carry):
            issue_scores(jnp.maximum(qi - 2 * p - 2, 0), 0)
            consume(qi - 2 * p - 1, 1, False)
            issue_scores(jnp.maximum(qi - 2 * p - 3, 0), 1)
            consume(qi - 2 * p - 2, 0, False)
            return carry

        lax.fori_loop(0, qi // 2, rest, 0)

        @pl.when(qi % 2 == 1)
        def _():
            consume(0, 1, False)

        _finish_head(acc_scr[...], g_ref, o_ref, qrows)
        return carry

    lax.fori_loop(0, nblk, query_block, 0)


def _attn_specs(s, qc, kc, vc):
    return [
        pl.BlockSpec((s, LANE), lambda h: (0, qc + h)),
        pl.BlockSpec((s, LANE), lambda h: (0, kc + h)),
        pl.BlockSpec((LANE, s), lambda h: (vc + h, 0)),
    ]


_ATTN_PARAMS = pltpu.CompilerParams(
    dimension_semantics=("arbitrary",), vmem_limit_bytes=VMEM_LIMIT)
_GCOL_SPEC = pl.BlockSpec((LANE, 1), lambda h: (0, 0))


def _fox(qk, vt, dtok, gcol, *, t):
    s = qk.shape[0]
    return pl.pallas_call(
        functools.partial(_fox_kernel, t=t, nblk=s // t),
        grid=(H_FOX,),
        in_specs=_attn_specs(s, QC, KC, VC) + [
            pl.BlockSpec((s, LANE), lambda h: (0, 0)),
            _GCOL_SPEC,
        ],
        out_specs=pl.BlockSpec((s, LANE), lambda h: (0, h)),
        out_shape=jax.ShapeDtypeStruct((s, H_FOX * LANE), BF16),
        scratch_shapes=[
            pltpu.VMEM((s, 2 * LANE), BF16),
            pltpu.VMEM((2, t, t), F32),
            pltpu.VMEM((1, t), F32),
            pltpu.VMEM((HEAD_DIM + BF16_ROWS, t), F32),
        ],
        compiler_params=_ATTN_PARAMS,
        name="fox_attn",
    )(qk, qk, vt, dtok, gcol)


def _diff(qk, vt, slopes, lam, gcol, *, t):
    s = qk.shape[0]
    row = pltpu.VMEM((1, t), F32)
    acc = pltpu.VMEM((HEAD_DIM + BF16_ROWS, t), F32)
    smem = pl.BlockSpec(memory_space=pltpu.SMEM)
    return pl.pallas_call(
        functools.partial(_diff_kernel, t=t, nblk=s // t),
        grid=(H_DIFF,),
        in_specs=[smem, smem] + _attn_specs(s, QA, KA, VA) + [_GCOL_SPEC],
        out_specs=pl.BlockSpec((s, LANE), lambda h: (0, h)),
        out_shape=jax.ShapeDtypeStruct((s, H_DIFF * LANE), BF16),
        scratch_shapes=[
            pltpu.VMEM((s, 2 * LANE), BF16),
            pltpu.VMEM((2, 2, t, t), F32),
            row, acc, row, acc,
        ],
        compiler_params=_ATTN_PARAMS,
        name="diff_attn",
    )(slopes, lam, qk, qk, vt, gcol)


def _sb(qk, vt, gcol, *, t, tc):
    s = qk.shape[0]
    return pl.pallas_call(
        functools.partial(_sb_kernel, t=t, tc=tc, nblk=s // t),
        grid=(H_SB,),
        in_specs=_attn_specs(s, QB, KB, VB) + [_GCOL_SPEC],
        out_specs=pl.BlockSpec((s, LANE), lambda h: (0, h)),
        out_shape=jax.ShapeDtypeStruct((s, H_SB * LANE), BF16),
        scratch_shapes=[
            pltpu.VMEM((2, t, t), F32),
            pltpu.VMEM((1, t), F32),
            pltpu.VMEM((HEAD_DIM, t), F32),
        ],
        compiler_params=_ATTN_PARAMS,
        name="sb_attn",
    )(qk, qk, vt, gcol)


def _route(logits_t, bias):
    score = [jax.nn.sigmoid(logits_t[e:e + 1, :]) for e in range(N_EXPERTS)]
    sel = [score[e] + bias[e:e + 1, :] for e in range(N_EXPERTS)]
    gscore = []
    for g in range(N_GROUPS):
        mem = sel[g * EXPERTS_PER_GROUP:(g + 1) * EXPERTS_PER_GROUP]
        best = None
        for a in range(EXPERTS_PER_GROUP):
            for b in range(a + 1, EXPERTS_PER_GROUP):
                pair = mem[a] + mem[b]
                best = pair if best is None else jnp.maximum(best, pair)
        gscore.append(best)
    gmax = functools.reduce(jnp.maximum, gscore)
    taken = None
    in_group = []
    for g in range(N_GROUPS):
        hit = gscore[g] == gmax
        if taken is not None:
            hit = hit & ~taken
        taken = hit if taken is None else taken | hit
        in_group.append(hit)
    masked = [jnp.where(in_group[e // EXPERTS_PER_GROUP], sel[e], -jnp.inf) for e in range(N_EXPERTS)]

    def first_max(vals):
        top = functools.reduce(jnp.maximum, vals)
        seen = None
        picks = []
        for v in vals:
            hit = v == top
            if seen is not None:
                hit = hit & ~seen
            seen = hit if seen is None else seen | hit
            picks.append(hit)
        return picks

    pick1 = first_max(masked)
    rest = [jnp.where(pick1[e], -jnp.inf, masked[e]) for e in range(N_EXPERTS)]
    pick2 = first_max(rest)
    chosen = [pick1[e] | pick2[e] for e in range(N_EXPERTS)]
    gate = [jnp.where(chosen[e], score[e], 0.0) for e in range(N_EXPERTS)]
    total = functools.reduce(lambda a, b: a + b, gate)
    return jnp.concatenate([gt / total for gt in gate], axis=0)


def _outproj_kernel(x_ref, oa_ref, ob_ref, oc_ref, wa_ref, wb_ref, wc_ref, g_ref,
                    wrh_ref, wrl_ref, rb_ref, h_ref, hn_ref, comb_ref, *, tm):
    h = x_ref[...] + _dot(oa_ref[...], wa_ref[...]) + _dot(ob_ref[...], wb_ref[...]) \
        + _dot(oc_ref[...], wc_ref[...])
    h_ref[...] = h
    ms = jnp.mean(h * h, axis=-1, keepdims=True)
    hn = h * lax.rsqrt(ms + EPS) * g_ref[...]
    hi, lo = _split_bf16(hn)
    hn_ref[...] = hi
    wrh = wrh_ref[...]
    logits_t = _dot_nt(wrh, hi) + _dot_nt(wrh, lo) + _dot_nt(wrl_ref[...], hi)
    comb = _route(logits_t, rb_ref[...])
    pad = jnp.zeros((LANE - N_EXPERTS, tm), F32)
    comb_ref[...] = jnp.concatenate([comb, pad], axis=0).T


def _outproj(x, oa, ob, oc, wa, wb, wc, g, wrh, wrl, rb, *, tm):
    s = x.shape[0]
    row = lambda i: (i, 0)
    fixed = lambda i: (0, 0)
    return pl.pallas_call(
        functools.partial(_outproj_kernel, tm=tm),
        grid=(s // tm,),
        in_specs=[
            pl.BlockSpec((tm, D_MODEL), row),
            pl.BlockSpec((tm, H_DIFF * LANE), row),
            pl.BlockSpec((tm, H_SB * LANE), row),
            pl.BlockSpec((tm, H_FOX * LANE), row),
            pl.BlockSpec((H_DIFF * LANE, D_MODEL), fixed, pipeline_mode=pl.Buffered(1)),
            pl.BlockSpec((H_SB * LANE, D_MODEL), fixed, pipeline_mode=pl.Buffered(1)),
            pl.BlockSpec((H_FOX * LANE, D_MODEL), fixed, pipeline_mode=pl.Buffered(1)),
            pl.BlockSpec((1, D_MODEL), fixed),
            pl.BlockSpec((N_EXPERTS, D_MODEL), fixed),
            pl.BlockSpec((N_EXPERTS, D_MODEL), fixed),
            pl.BlockSpec((N_EXPERTS, 1), fixed),
        ],
        out_specs=[
            pl.BlockSpec((tm, D_MODEL), row),
            pl.BlockSpec((tm, D_MODEL), row),
            pl.BlockSpec((tm, LANE), row),
        ],
        out_shape=[
            jax.ShapeDtypeStruct((s, D_MODEL), F32),
            jax.ShapeDtypeStruct((s, D_MODEL), BF16),
            jax.ShapeDtypeStruct((s, LANE), F32),
        ],
        compiler_params=pltpu.CompilerParams(
            dimension_semantics=("arbitrary",), vmem_limit_bytes=VMEM_LIMIT),
        name="outproj_router",
    )(x, oa, ob, oc, wa, wb, wc, g, wrh, wrl, rb)


def _moe_kernel(h_ref, hn_ref, comb_ref, wg_ref, wu_ref, wd_ref, o_ref):
    e = pl.program_id(1)

    @pl.when(e == 0)
    def _():
        o_ref[...] = h_ref[...]

    hn = hn_ref[...]
    comb = comb_ref[...]
    lane = lax.broadcasted_iota(jnp.int32, comb.shape, 1)
    weight = jnp.sum(jnp.where(lane == e, comb, 0.0), axis=-1, keepdims=True)
    gate = _dot(hn, wg_ref[...])
    up = _dot(hn, wu_ref[...])
    act = gate * jax.nn.sigmoid(gate) * up * weight
    o_ref[...] += _dot(act.astype(BF16), wd_ref[...])


def _moe(h, hn, comb, wg, wu, wd, *, tm):
    s = h.shape[0]
    row = lambda i, e: (i, 0)
    return pl.pallas_call(
        _moe_kernel,
        grid=(s // tm, N_EXPERTS),
        in_specs=[
            pl.BlockSpec((tm, D_MODEL), row),
            pl.BlockSpec((tm, D_MODEL), row),
            pl.BlockSpec((tm, LANE), row),
            pl.BlockSpec((None, D_MODEL, D_EXPERT), lambda i, e: (e, 0, 0)),
            pl.BlockSpec((None, D_MODEL, D_EXPERT), lambda i, e: (e, 0, 0)),
            pl.BlockSpec((None, D_EXPERT, D_MODEL), lambda i, e: (e, 0, 0)),
        ],
        out_specs=pl.BlockSpec((tm, D_MODEL), row),
        out_shape=jax.ShapeDtypeStruct((s, D_MODEL), F32),
        compiler_params=pltpu.CompilerParams(
            dimension_semantics=("arbitrary", "arbitrary"), vmem_limit_bytes=VMEM_LIMIT),
        name="moe_experts",
    )(h, hn, comb, wg, wu, wd)


def _proj_epilogue_tables(qnorm_diff, knorm_diff, qnorm_fox, knorm_fox):
    ones = jnp.ones((HEAD_DIM,), F32)
    normed = jnp.full((HEAD_DIM,), EPS, F32)
    plain = ones
    half = jnp.arange(HEAD_DIM) // DIFF_HALF
    g_diff = (half[:, None] == half[None, :]).astype(F32) / DIFF_HALF
    g_full = jnp.full((HEAD_DIM, HEAD_DIM), 1.0 / HEAD_DIM, F32)
    g_none = jnp.zeros((HEAD_DIM, HEAD_DIM), F32)
    diff_scale = DIFF_HALF ** -0.5 * LOG2E
    full_scale = HEAD_DIM ** -0.5 * LOG2E
    groups = [
        (H_DIFF, g_diff, normed, jnp.tile(qnorm_diff, 2) * diff_scale),
        (H_DIFF, g_diff, normed, jnp.tile(knorm_diff, 2)),
        (H_SB, g_none, plain, ones * full_scale),
        (H_SB, g_none, plain, ones),
        (H_FOX, g_full, normed, qnorm_fox * full_scale),
        (H_FOX, g_full, normed, knorm_fox),
    ]
    gmat = jnp.concatenate([jnp.broadcast_to(gm, (n, HEAD_DIM, HEAD_DIM)) for n, gm, _, _ in groups])
    eps = jnp.concatenate([jnp.tile(ep, n) for n, _, ep, _ in groups])[None, :]
    gain = jnp.concatenate([jnp.tile(gn, n) for n, _, _, gn in groups])[None, :]
    return gmat.astype(BF16), eps, gain


def _split_in_weights(w):
    wa, wb, wc = H_DIFF * HEAD_DIM, H_SB * HEAD_DIM, H_FOX * HEAD_DIM
    pieces = []
    off = 0
    for width in (wa, wa, wa, wb, wb, wb, wc, wc, wc):
        pieces.append(w[:, off:off + width])
        off += width
    qa, ka, va, qb, kb, vb, qc, kc, vc = pieces
    return (jnp.concatenate([qa, ka, qb, kb, qc, kc], axis=1),
            jnp.concatenate([va, vb, vc], axis=1), w[:, off:])


def _tiles(s):
    return dict(proj_tm=min(256, s), proj_tn=512, attn_t=min(512, s), sb_tc=256,
                out_tm=min(512, s), moe_tm=min(512, s))


def kernel(x, attn_norm, w_in, b_forget, qnorm_diff, knorm_diff, lam_q1, lam_k1, lam_q2, lam_k2,
           subln_diff, onorm_sb, qnorm_fox, knorm_fox, onorm_fox, w_out, ffn_norm,
           w_router, router_bias, w_gate, w_up, w_down):
    b, s, d = x.shape
    assert b == 1 and d == D_MODEL
    depth = w_in.shape[0]
    tl = _tiles(s)
    t = tl["attn_t"]
    slopes = jnp.exp2(-8.0 * jnp.arange(1, H_DIFF + 1, dtype=F32) / H_DIFF) * LOG2E
    wr_t = w_router.T.astype(F32)
    wrh = wr_t.astype(BF16)
    wrl = (wr_t - wrh.astype(F32)).astype(BF16)
    rb = router_bias.astype(F32)[:, None]
    xs = x[0]
    w_in_b = lax.optimization_barrier(w_in.astype(BF16))
    for l in range(depth):
        w_qk, w_v, w_fc = _split_in_weights(w_in_b[l])
        wfc_t = jnp.zeros((LANE, D_MODEL), BF16).at[:H_FOX].set(w_fc.T)
        bfc = jnp.zeros((LANE, 1), F32).at[:H_FOX, 0].set(b_forget[l])
        gmat, eps, gain = _proj_epilogue_tables(qnorm_diff[l], knorm_diff[l], qnorm_fox[l], knorm_fox[l])
        qk, vt, dtok = _proj(xs, attn_norm[l][None, :], w_qk, w_v.T, wfc_t, bfc, gmat, eps, gain,
                             tm=tl["proj_tm"], tn=tl["proj_tn"])

        lam_init = 0.8 - 0.6 * math.exp(-0.3 * l)
        lam = (jnp.exp(jnp.sum(lam_q1[l] * lam_k1[l])) - jnp.exp(jnp.sum(lam_q2[l] * lam_k2[l]))
               + lam_init).reshape(1).astype(F32)
        oa = _diff(qk, vt, slopes, lam, (subln_diff[l] * (1.0 - lam_init))[:, None], t=t)
        ob = _sb(qk, vt, onorm_sb[l][:, None], t=t, tc=min(tl["sb_tc"], t))
        oc = _fox(qk, vt, dtok, onorm_fox[l][:, None], t=t)

        wo = w_out[l].astype(BF16)
        na, nb = H_DIFF * LANE, (H_DIFF + H_SB) * LANE
        h, hn, comb = _outproj(xs, oa, ob, oc, wo[:na], wo[na:nb], wo[nb:], ffn_norm[l][None, :],
                               wrh, wrl, rb, tm=tl["out_tm"])
        xs = _moe(h, hn, comb, w_gate[l].astype(BF16), w_up[l].astype(BF16), w_down[l].astype(BF16),
                  tm=tl["moe_tm"])
    return xs[None]
```

```python
import functools
import math

import jax
import jax.numpy as jnp
from jax import lax
from jax.experimental import pallas as pl
from jax.experimental.pallas import tpu as pltpu

F32 = jnp.float32
BF16 = jnp.bfloat16

D_MODEL = 2048
HEAD_DIM = 128
LANE = 128
BF16_ROWS = 16
N_HEADS = 16
H_SB = N_HEADS // 3
H_FOX = N_HEADS // 3
H_DIFF = N_HEADS - H_SB - H_FOX
DIFF_HALF = HEAD_DIM // 2
CHUNK = 64
N_EXPERTS = 16
N_GROUPS = 4
EXPERTS_PER_GROUP = N_EXPERTS // N_GROUPS
D_EXPERT = D_MODEL // 4
EPS = 1e-6
LOG2E = 1.4426950408889634

QA, KA = 0, H_DIFF
QB, KB = 2 * H_DIFF, 2 * H_DIFF + H_SB
QC, KC = 2 * H_DIFF + 2 * H_SB, 2 * H_DIFF + 2 * H_SB + H_FOX
N_QK = 2 * N_HEADS * HEAD_DIM
VA, VB, VC = 0, H_DIFF, H_DIFF + H_SB
N_V = N_HEADS * HEAD_DIM

VMEM_LIMIT = 48 * 1024 * 1024

_NT = (((1,), (1,)), ((), ()))


def _dot(a, b):
    return jnp.dot(a, b, preferred_element_type=F32)


def _dot_nt(a, b):
    return lax.dot_general(a, b, _NT, preferred_element_type=F32)


def _log_sigmoid(x):
    return jnp.minimum(x, 0.0) - jnp.log1p(jnp.exp(-jnp.abs(x)))


def _split_bf16(x):
    hi = x.astype(BF16)
    lo = (x - hi.astype(F32)).astype(BF16)
    return hi, lo


def _split3_bf16(x):
    hi = x.astype(BF16)
    r = x - hi.astype(F32)
    mid = r.astype(BF16)
    lo = (r - mid.astype(F32)).astype(BF16)
    return hi, mid, lo


def _proj_kernel(x_ref, g_ref, wqk_ref, wv_ref, wfc_ref, bfc_ref, gmat_ref, eps_ref, gain_ref,
                 qk_ref, vt_ref, dtok_ref, carry_scr, *, tm, tn):
    x = x_ref[...]
    ms = jnp.mean(x * x, axis=-1, keepdims=True)
    xn = (x * lax.rsqrt(ms + EPS) * g_ref[...]).astype(BF16)

    @pl.when(pl.program_id(0) == 0)
    def _():
        carry_scr[...] = jnp.zeros_like(carry_scr)

    lf = _log_sigmoid(_dot_nt(wfc_ref[...], xn) + bfc_ref[...])
    hi, lo = _split_bf16(lf)
    r = lax.broadcasted_iota(jnp.int32, (tm, tm), 0)
    c = lax.broadcasted_iota(jnp.int32, (tm, tm), 1)
    tri = (r <= c).astype(BF16)
    cum = _dot(hi, tri) + _dot(lo, tri) + carry_scr[...]
    carry_scr[...] = cum[:, tm - 1:tm]
    dtok_ref[...] = (cum * LOG2E).T

    def project(j):
        return _dot(xn, wqk_ref[:, pl.ds(pl.multiple_of(j * tn, tn), tn)])

    def epilogue(j, acc):
        for c in range(tn // LANE):
            cols = pl.ds(pl.multiple_of(j * tn + c * LANE, LANE), LANE)
            y = acc[:, c * LANE:(c + 1) * LANE]
            ss = _dot((y * y).astype(BF16), gmat_ref[j * (tn // LANE) + c])
            qk_ref[:, cols] = (y * lax.rsqrt(ss + eps_ref[:, cols]) * gain_ref[:, cols]).astype(BF16)

    def qk_block(j, acc):
        nxt = project(j + 1)
        epilogue(j, acc)
        return nxt

    last = N_QK // tn - 1
    epilogue(last, lax.fori_loop(0, last, qk_block, project(0)))

    def v_block(j, carry):
        rows = pl.ds(pl.multiple_of(j * tn, tn), tn)
        vt_ref[rows, :] = _dot(xn, wv_ref[:, rows]).T.astype(BF16)
        return carry

    lax.fori_loop(0, N_V // tn, v_block, 0)


def _proj(x, g, wqk, wv, wfc_t, bfc, gmat, eps, gain, *, tm, tn):
    s = x.shape[0]
    fixed2 = lambda i: (0, 0)
    resident = pl.Buffered(1)
    return pl.pallas_call(
        functools.partial(_proj_kernel, tm=tm, tn=tn),
        grid=(s // tm,),
        in_specs=[
            pl.BlockSpec((tm, D_MODEL), lambda i: (i, 0)),
            pl.BlockSpec((1, D_MODEL), fixed2),
            pl.BlockSpec((D_MODEL, N_QK), fixed2, pipeline_mode=resident),
            pl.BlockSpec((D_MODEL, N_V), fixed2, pipeline_mode=resident),
            pl.BlockSpec((LANE, D_MODEL), fixed2),
            pl.BlockSpec((LANE, 1), fixed2),
            pl.BlockSpec((N_QK // LANE, LANE, LANE), lambda i: (0, 0, 0), pipeline_mode=resident),
            pl.BlockSpec((1, N_QK), fixed2),
            pl.BlockSpec((1, N_QK), fixed2),
        ],
        out_specs=[
            pl.BlockSpec((tm, N_QK), lambda i: (i, 0)),
            pl.BlockSpec((N_V, tm), lambda i: (0, i)),
            pl.BlockSpec((tm, LANE), lambda i: (i, 0)),
        ],
        out_shape=[
            jax.ShapeDtypeStruct((s, N_QK), BF16),
            jax.ShapeDtypeStruct((N_V, s), BF16),
            jax.ShapeDtypeStruct((s, LANE), F32),
        ],
        scratch_shapes=[pltpu.VMEM((LANE, 1), F32)],
        compiler_params=pltpu.CompilerParams(
            dimension_semantics=("arbitrary",), vmem_limit_bytes=VMEM_LIMIT),
        name="proj",
    )(x, g, wqk, wv, wfc_t, bfc, gmat, eps, gain)


def _bias_lanes(split, ones_value, *, split_first):
    hi, mid, lo = split
    lane = lax.broadcasted_iota(jnp.int32, hi.shape, 1)
    s0, c0 = (0, 3) if split_first else (3, 0)
    const = jnp.where((lane >= c0) & (lane < c0 + 3), ones_value, 0.0).astype(BF16)
    return jnp.where(lane == s0, hi, jnp.where(lane == s0 + 1, mid, jnp.where(lane == s0 + 2, lo, const)))


def _kq_iotas(t):
    return (lax.broadcasted_iota(jnp.int32, (t, t), 0),
            lax.broadcasted_iota(jnp.int32, (t, t), 1))


def _fold_rows(x, op):
    while x.shape[0] > 8:
        half = x.shape[0] // 2
        x = op(x[:half], x[half:])
    return x


def _softmax_step(s, vt, m_scr, acc_scr):
    m_old = m_scr[...]
    m_new = jnp.maximum(m_old, jnp.max(_fold_rows(s, jnp.maximum), axis=0, keepdims=True))
    alpha = jnp.exp2(m_old - m_new)
    p = jnp.exp2(s - m_new).astype(BF16)
    vt_aug = jnp.concatenate([vt, jnp.ones((BF16_ROWS, vt.shape[1]), BF16)], axis=0)
    acc_scr[...] = alpha * acc_scr[...] + _dot(vt_aug, p)
    m_scr[...] = m_new


def _finish_head(o_t, gcol_ref, o_ref, rows):
    ms = jnp.mean(o_t * o_t, axis=0, keepdims=True)
    o_ref[rows, :] = (o_t * lax.rsqrt(ms + EPS) * gcol_ref[...]).T.astype(BF16)


def _pipelined_blocks(qi, issue_scores, consume):
    issue_scores(0, 0)

    def body(p, carry):
        issue_scores(2 * p + 1, 1)
        consume(2 * p, 0, False)
        issue_scores(2 * p + 2, 0)
        consume(2 * p + 1, 1, False)
        return carry

    lax.fori_loop(0, qi // 2, body, 0)

    @pl.when(qi % 2 == 0)
    def _():
        consume(qi, 0, True)

    @pl.when(qi % 2 == 1)
    def _():
        issue_scores(qi, 1)
        consume(qi - 1, 0, False)
        consume(qi, 1, True)


def _fox_kernel(q_ref, k_ref, vt_ref, d_ref, g_ref, o_ref, kaug_scr, s_scr, m_scr, acc_scr, *, t, nblk):
    h = pl.program_id(0)
    pick = (lax.broadcasted_iota(jnp.int32, (LANE, LANE), 0) == h).astype(BF16)

    def head_bias(rows):
        return tuple(_dot(p, pick).astype(BF16) for p in _split3_bf16(d_ref[rows, :]))

    def build(b, carry):
        rows = pl.ds(pl.multiple_of(b * t, t), t)
        kaug_scr[rows, :LANE] = k_ref[rows, :]
        kaug_scr[rows, LANE:] = _bias_lanes(head_bias(rows), 1.0, split_first=True)
        return carry

    lax.fori_loop(0, nblk, build, 0)

    def query_block(qi, carry):
        qrows = pl.ds(pl.multiple_of(qi * t, t), t)
        q_aug = jnp.concatenate(
            [q_ref[qrows, :], _bias_lanes(head_bias(qrows), -1.0, split_first=False)], axis=1)
        m_scr[...] = jnp.full_like(m_scr, -jnp.inf)
        acc_scr[...] = jnp.zeros_like(acc_scr)

        def issue_scores(j, slot):
            s_scr[slot] = _dot_nt(kaug_scr[pl.ds(pl.multiple_of(j * t, t), t), :], q_aug)

        def consume(j, slot, masked):
            s = s_scr[slot]
            if masked:
                kpos, qpos = _kq_iotas(t)
                s = jnp.where(kpos <= qpos, s, -jnp.inf)
            _softmax_step(s, vt_ref[:, pl.ds(pl.multiple_of(j * t, t), t)], m_scr, acc_scr)

        _pipelined_blocks(qi, issue_scores, consume)
        acc = acc_scr[...]
        _finish_head(acc[:HEAD_DIM] / acc[HEAD_DIM:HEAD_DIM + 1], g_ref, o_ref, qrows)
        return carry

    lax.fori_loop(0, nblk, query_block, 0)


def _diff_kernel(slope_ref, lam_ref, q_ref, k_ref, vt_ref, g_ref, o_ref,
                 kaug_scr, s_scr, m1_scr, a1_scr, m2_scr, a2_scr, *, t, nblk):
    h = pl.program_id(0)
    slope = slope_ref[h]
    lam = lam_ref[0]

    def pos_split(block, sign):
        pos = (block * t + lax.broadcasted_iota(jnp.int32, (t, LANE), 0)).astype(F32)
        return _split3_bf16(sign * slope * pos)

    def build(b, carry):
        rows = pl.ds(pl.multiple_of(b * t, t), t)
        kaug_scr[rows, :LANE] = k_ref[rows, :]
        kaug_scr[rows, LANE:] = _bias_lanes(pos_split(b, 1.0), 1.0, split_first=True)
        return carry

    lax.fori_loop(0, nblk, build, 0)

    def query_block(qi, carry):
        qrows = pl.ds(pl.multiple_of(qi * t, t), t)
        q = q_ref[qrows, :]
        lane = lax.broadcasted_iota(jnp.int32, q.shape, 1)
        zero = jnp.zeros_like(q)
        q_bias = _bias_lanes(pos_split(qi, -1.0), 1.0, split_first=False)
        q1_aug = jnp.concatenate([jnp.where(lane < DIFF_HALF, q, zero), q_bias], axis=1)
        q2_aug = jnp.concatenate([jnp.where(lane >= DIFF_HALF, q, zero), q_bias], axis=1)
        for m_scr, a_scr in ((m1_scr, a1_scr), (m2_scr, a2_scr)):
            m_scr[...] = jnp.full_like(m_scr, -jnp.inf)
            a_scr[...] = jnp.zeros_like(a_scr)

        def issue_scores(j, slot):
            kb = kaug_scr[pl.ds(pl.multiple_of(j * t, t), t), :]
            s_scr[slot, 0] = _dot_nt(kb, q1_aug)
            s_scr[slot, 1] = _dot_nt(kb, q2_aug)

        def consume(j, slot, masked):
            vt = vt_ref[:, pl.ds(pl.multiple_of(j * t, t), t)]
            for idx, (m_scr, a_scr) in enumerate(((m1_scr, a1_scr), (m2_scr, a2_scr))):
                s = s_scr[slot, idx]
                if masked:
                    kpos, qpos = _kq_iotas(t)
                    ahead = (kpos - qpos).astype(F32)
                    s = s - jnp.where(ahead > 0.0, (2.0 * slope) * ahead, 0.0)
                    s = jnp.where((kpos // CHUNK) <= (qpos // CHUNK), s, -jnp.inf)
                _softmax_step(s, vt, m_scr, a_scr)

        _pipelined_blocks(qi, issue_scores, consume)
        a1 = a1_scr[...]
        a2 = a2_scr[...]
        o_t = (a1[:HEAD_DIM] / a1[HEAD_DIM:HEAD_DIM + 1]
               - lam * (a2[:HEAD_DIM] / a2[HEAD_DIM:HEAD_DIM + 1]))
        _finish_head(o_t, g_ref, o_ref, qrows)
        return carry

    lax.fori_loop(0, nblk, query_block, 0)


def _sb_kernel(q_ref, k_ref, vt_ref, g_ref, o_ref, s_scr, run_scr, acc_scr, *, t, tc, nblk):
    ur = lax.broadcasted_iota(jnp.int32, (tc, tc), 0)
    uc = lax.broadcasted_iota(jnp.int32, (tc, tc), 1)
    later = jnp.concatenate([(uc > ur).astype(BF16), jnp.ones((BF16_ROWS, tc), BF16)], axis=0)
    nchunk = t // tc

    def query_block(qi, carry):
        qrows = pl.ds(pl.multiple_of(qi * t, t), t)
        q = q_ref[qrows, :]
        run_scr[...] = jnp.zeros_like(run_scr)
        acc_scr[...] = jnp.zeros_like(acc_scr)

        def issue_scores(j, slot):
            s_scr[slot] = _dot_nt(k_ref[pl.ds(pl.multiple_of(j * t, t), t), :], q)

        def consume(j, slot, masked):
            z = s_scr[slot]
            log_beta = jnp.minimum(z, 0.0) - jnp.log2(1.0 + jnp.exp2(-jnp.abs(z)))
            log_keep = log_beta - z
            if masked:
                kpos, qpos = _kq_iotas(t)
                strict = kpos < qpos
                log_keep = jnp.where(strict, log_keep, 0.0)
            run = run_scr[...]
            acc = acc_scr[...]
            for ci in range(nchunk - 1, -1, -1):
                sl = slice(ci * tc, (ci + 1) * tc)
                sums = _dot(later, log_keep[sl].astype(BF16))
                a = jnp.exp2(log_beta[sl] + sums[:tc])
                if masked:
                    a = jnp.where(strict[sl], a, 0.0)
                cols = pl.ds(pl.multiple_of(j * t + ci * tc, tc), tc)
                acc = acc + jnp.exp2(run) * _dot(vt_ref[:, cols], a.astype(BF16))
                run = run + sums[tc:tc + 1]
            acc_scr[...] = acc
            run_scr[...] = run

        issue_scores(qi, 0)
        issue_scores(jnp.maximum(qi - 1, 0), 1)
        consume(qi, 0, True)

        def rest(p, carry):
            issue_scores(jnp.maximum(qi - 2 * p - 2, 0), 0)
            consume(qi - 2 * p - 1, 1, False)
            issue_scores(jnp.maximum(qi - 2 * p - 3, 0), 1)
            consume(qi - 2 * p - 2, 0, False)
            return carry

        lax.fori_loop(0, qi // 2, rest, 0)

        @pl.when(qi % 2 == 1)
        def _():
            consume(0, 1, False)

        _finish_head(acc_scr[...], g_ref, o_ref, qrows)
        return carry

    lax.fori_loop(0, nblk, query_block, 0)


def _attn_specs(s, qc, kc, vc):
    return [
        pl.BlockSpec((s, LANE), lambda h: (0, qc + h)),
        pl.BlockSpec((s, LANE), lambda h: (0, kc + h)),
        pl.BlockSpec((LANE, s), lambda h: (vc + h, 0)),
    ]


_ATTN_PARAMS = pltpu.CompilerParams(
    dimension_semantics=("arbitrary",), vmem_limit_bytes=VMEM_LIMIT)
_GCOL_SPEC = pl.BlockSpec((LANE, 1), lambda h: (0, 0))


def _fox(qk, vt, dtok, gcol, *, t):
    s = qk.shape[0]
    return pl.pallas_call(
        functools.partial(_fox_kernel, t=t, nblk=s // t),
        grid=(H_FOX,),
        in_specs=_attn_specs(s, QC, KC, VC) + [
            pl.BlockSpec((s, LANE), lambda h: (0, 0)),
            _GCOL_SPEC,
        ],
        out_specs=pl.BlockSpec((s, LANE), lambda h: (0, h)),
        out_shape=jax.ShapeDtypeStruct((s, H_FOX * LANE), BF16),
        scratch_shapes=[
            pltpu.VMEM((s, 2 * LANE), BF16),
            pltpu.VMEM((2, t, t), F32),
            pltpu.VMEM((1, t), F32),
            pltpu.VMEM((HEAD_DIM + BF16_ROWS, t), F32),
        ],
        compiler_params=_ATTN_PARAMS,
        name="fox_attn",
    )(qk, qk, vt, dtok, gcol)


def _diff(qk, vt, slopes, lam, gcol, *, t):
    s = qk.shape[0]
    row = pltpu.VMEM((1, t), F32)
    acc = pltpu.VMEM((HEAD_DIM + BF16_ROWS, t), F32)
    smem = pl.BlockSpec(memory_space=pltpu.SMEM)
    return pl.pallas_call(
        functools.partial(_diff_kernel, t=t, nblk=s // t),
        grid=(H_DIFF,),
        in_specs=[smem, smem] + _attn_specs(s, QA, KA, VA) + [_GCOL_SPEC],
        out_specs=pl.BlockSpec((s, LANE), lambda h: (0, h)),
        out_shape=jax.ShapeDtypeStruct((s, H_DIFF * LANE), BF16),
        scratch_shapes=[
            pltpu.VMEM((s, 2 * LANE), BF16),
            pltpu.VMEM((2, 2, t, t), F32),
            row, acc, row, acc,
        ],
        compiler_params=_ATTN_PARAMS,
        name="diff_attn",
    )(slopes, lam, qk, qk, vt, gcol)


def _sb(qk, vt, gcol, *, t, tc):
    s = qk.shape[0]
    return pl.pallas_call(
        functools.partial(_sb_kernel, t=t, tc=tc, nblk=s // t),
        grid=(H_SB,),
        in_specs=_attn_specs(s, QB, KB, VB) + [_GCOL_SPEC],
        out_specs=pl.BlockSpec((s, LANE), lambda h: (0, h)),
        out_shape=jax.ShapeDtypeStruct((s, H_SB * LANE), BF16),
        scratch_shapes=[
            pltpu.VMEM((2, t, t), F32),
            pltpu.VMEM((1, t), F32),
            pltpu.VMEM((HEAD_DIM, t), F32),
        ],
        compiler_params=_ATTN_PARAMS,
        name="sb_attn",
    )(qk, qk, vt, gcol)


def _route(logits_t, bias):
    score = [jax.nn.sigmoid(logits_t[e:e + 1, :]) for e in range(N_EXPERTS)]
    sel = [score[e] + bias[e:e + 1, :] for e in range(N_EXPERTS)]
    gscore = []
    for g in range(N_GROUPS):
        mem = sel[g * EXPERTS_PER_GROUP:(g + 1) * EXPERTS_PER_GROUP]
        best = None
        for a in range(EXPERTS_PER_GROUP):
            for b in range(a + 1, EXPERTS_PER_GROUP):
                pair = mem[a] + mem[b]
                best = pair if best is None else jnp.maximum(best, pair)
        gscore.append(best)
    gmax = functools.reduce(jnp.maximum, gscore)
    taken = None
    in_group = []
    for g in range(N_GROUPS):
        hit = gscore[g] == gmax
        if taken is not None:
            hit = hit & ~taken
        taken = hit if taken is None else taken | hit
        in_group.append(hit)
    masked = [jnp.where(in_group[e // EXPERTS_PER_GROUP], sel[e], -jnp.inf) for e in range(N_EXPERTS)]

    def first_max(vals):
        top = functools.reduce(jnp.maximum, vals)
        seen = None
        picks = []
        for v in vals:
            hit = v == top
            if seen is not None:
                hit = hit & ~seen
            seen = hit if seen is None else seen | hit
            picks.append(hit)
        return picks

    pick1 = first_max(masked)
    rest = [jnp.where(pick1[e], -jnp.inf, masked[e]) for e in range(N_EXPERTS)]
    pick2 = first_max(rest)
    chosen = [pick1[e] | pick2[e] for e in range(N_EXPERTS)]
    gate = [jnp.where(chosen[e], score[e], 0.0) for e in range(N_EXPERTS)]
    total = functools.reduce(lambda a, b: a + b, gate)
    return jnp.concatenate([gt / total for gt in gate], axis=0)


def _outproj_kernel(x_ref, oa_ref, ob_ref, oc_ref, wa_ref, wb_ref, wc_ref, g_ref,
                    wrh_ref, wrl_ref, rb_ref, h_ref, hn_ref, comb_ref, *, tm):
    h = x_ref[...] + _dot(oa_ref[...], wa_ref[...]) + _dot(ob_ref[...], wb_ref[...]) \
        + _dot(oc_ref[...], wc_ref[...])
    h_ref[...] = h
    ms = jnp.mean(h * h, axis=-1, keepdims=True)
    hn = h * lax.rsqrt(ms + EPS) * g_ref[...]
    hi, lo = _split_bf16(hn)
    hn_ref[...] = hi
    wrh = wrh_ref[...]
    logits_t = _dot_nt(wrh, hi) + _dot_nt(wrh, lo) + _dot_nt(wrl_ref[...], hi)
    comb = _route(logits_t, rb_ref[...])
    pad = jnp.zeros((LANE - N_EXPERTS, tm), F32)
    comb_ref[...] = jnp.concatenate([comb, pad], axis=0).T


def _outproj(x, oa, ob, oc, wa, wb, wc, g, wrh, wrl, rb, *, tm):
    s = x.shape[0]
    row = lambda i: (i, 0)
    fixed = lambda i: (0, 0)
    return pl.pallas_call(
        functools.partial(_outproj_kernel, tm=tm),
        grid=(s // tm,),
        in_specs=[
            pl.BlockSpec((tm, D_MODEL), row),
            pl.BlockSpec((tm, H_DIFF * LANE), row),
            pl.BlockSpec((tm, H_SB * LANE), row),
            pl.BlockSpec((tm, H_FOX * LANE), row),
            pl.BlockSpec((H_DIFF * LANE, D_MODEL), fixed, pipeline_mode=pl.Buffered(1)),
            pl.BlockSpec((H_SB * LANE, D_MODEL), fixed, pipeline_mode=pl.Buffered(1)),
            pl.BlockSpec((H_FOX * LANE, D_MODEL), fixed, pipeline_mode=pl.Buffered(1)),
            pl.BlockSpec((1, D_MODEL), fixed),
            pl.BlockSpec((N_EXPERTS, D_MODEL), fixed),
            pl.BlockSpec((N_EXPERTS, D_MODEL), fixed),
            pl.BlockSpec((N_EXPERTS, 1), fixed),
        ],
        out_specs=[
            pl.BlockSpec((tm, D_MODEL), row),
            pl.BlockSpec((tm, D_MODEL), row),
            pl.BlockSpec((tm, LANE), row),
        ],
        out_shape=[
            jax.ShapeDtypeStruct((s, D_MODEL), F32),
            jax.ShapeDtypeStruct((s, D_MODEL), BF16),
            jax.ShapeDtypeStruct((s, LANE), F32),
        ],
        compiler_params=pltpu.CompilerParams(
            dimension_semantics=("arbitrary",), vmem_limit_bytes=VMEM_LIMIT),
        name="outproj_router",
    )(x, oa, ob, oc, wa, wb, wc, g, wrh, wrl, rb)


def _moe_kernel(h_ref, hn_ref, comb_ref, wg_ref, wu_ref, wd_ref, o_ref):
    e = pl.program_id(1)

    @pl.when(e == 0)
    def _():
        o_ref[...] = h_ref[...]

    hn = hn_ref[...]
    comb = comb_ref[...]
    lane = lax.broadcasted_iota(jnp.int32, comb.shape, 1)
    weight = jnp.sum(jnp.where(lane == e, comb, 0.0), axis=-1, keepdims=True)
    gate = _dot(hn, wg_ref[...])
    up = _dot(hn, wu_ref[...])
    act = gate * jax.nn.sigmoid(gate) * up * weight
    o_ref[...] += _dot(act.astype(BF16), wd_ref[...])


def _moe(h, hn, comb, wg, wu, wd, *, tm):
    s = h.shape[0]
    row = lambda i, e: (i, 0)
    return pl.pallas_call(
        _moe_kernel,
        grid=(s // tm, N_EXPERTS),
        in_specs=[
            pl.BlockSpec((tm, D_MODEL), row),
            pl.BlockSpec((tm, D_MODEL), row),
            pl.BlockSpec((tm, LANE), row),
            pl.BlockSpec((None, D_MODEL, D_EXPERT), lambda i, e: (e, 0, 0)),
            pl.BlockSpec((None, D_MODEL, D_EXPERT), lambda i, e: (e, 0, 0)),
            pl.BlockSpec((None, D_EXPERT, D_MODEL), lambda i, e: (e, 0, 0)),
        ],
        out_specs=pl.BlockSpec((tm, D_MODEL), row),
        out_shape=jax.ShapeDtypeStruct((s, D_MODEL), F32),
        compiler_params=pltpu.CompilerParams(
            dimension_semantics=("arbitrary", "arbitrary"), vmem_limit_bytes=VMEM_LIMIT),
        name="moe_experts",
    )(h, hn, comb, wg, wu, wd)


def _proj_epilogue_tables(qnorm_diff, knorm_diff, qnorm_fox, knorm_fox):
    ones = jnp.ones((HEAD_DIM,), F32)
    normed = jnp.full((HEAD_DIM,), EPS, F32)
    plain = ones
    half = jnp.arange(HEAD_DIM) // DIFF_HALF
    g_diff = (half[:, None] == half[None, :]).astype(F32) / DIFF_HALF
    g_full = jnp.full((HEAD_DIM, HEAD_DIM), 1.0 / HEAD_DIM, F32)
    g_none = jnp.zeros((HEAD_DIM, HEAD_DIM), F32)
    diff_scale = DIFF_HALF ** -0.5 * LOG2E
    full_scale = HEAD_DIM ** -0.5 * LOG2E
    groups = [
        (H_DIFF, g_diff, normed, jnp.tile(qnorm_diff, 2) * diff_scale),
        (H_DIFF, g_diff, normed, jnp.tile(knorm_diff, 2)),
        (H_SB, g_none, plain, ones * full_scale),
        (H_SB, g_none, plain, ones),
        (H_FOX, g_full, normed, qnorm_fox * full_scale),
        (H_FOX, g_full, normed, knorm_fox),
    ]
    gmat = jnp.concatenate([jnp.broadcast_to(gm, (n, HEAD_DIM, HEAD_DIM)) for n, gm, _, _ in groups])
    eps = jnp.concatenate([jnp.tile(ep, n) for n, _, ep, _ in groups])[None, :]
    gain = jnp.concatenate([jnp.tile(gn, n) for n, _, _, gn in groups])[None, :]
    return gmat.astype(BF16), eps, gain


def _split_in_weights(w):
    wa, wb, wc = H_DIFF * HEAD_DIM, H_SB * HEAD_DIM, H_FOX * HEAD_DIM
    pieces = []
    off = 0
    for width in (wa, wa, wa, wb, wb, wb, wc, wc, wc):
        pieces.append(w[:, off:off + width])
        off += width
    qa, ka, va, qb, kb, vb, qc, kc, vc = pieces
    return (jnp.concatenate([qa, ka, qb, kb, qc, kc], axis=1),
            jnp.concatenate([va, vb, vc], axis=1), w[:, off:])


def _tiles(s):
    return dict(proj_tm=min(256, s), proj_tn=512, attn_t=min(512, s), sb_tc=256,
                out_tm=min(512, s), moe_tm=min(512, s))


def kernel(x, attn_norm, w_in, b_forget, qnorm_diff, knorm_diff, lam_q1, lam_k1, lam_q2, lam_k2,
           subln_diff, onorm_sb, qnorm_fox, knorm_fox, onorm_fox, w_out, ffn_norm,
           w_router, router_bias, w_gate, w_up, w_down):
    b, s, d = x.shape
    assert b == 1 and d == D_MODEL
    depth = w_in.shape[0]
    tl = _tiles(s)
    t = tl["attn_t"]
    slopes = jnp.exp2(-8.0 * jnp.arange(1, H_DIFF + 1, dtype=F32) / H_DIFF) * LOG2E
    wr_t = w_router.T.astype(F32)
    wrh = wr_t.astype(BF16)
    wrl = (wr_t - wrh.astype(F32)).astype(BF16)
    rb = router_bias.astype(F32)[:, None]
    xs = x[0]
    w_in_b = lax.optimization_barrier(w_in.astype(BF16))
    for l in range(depth):
        w_qk, w_v, w_fc = _split_in_weights(w_in_b[l])
        wfc_t = jnp.zeros((LANE, D_MODEL), BF16).at[:H_FOX].set(w_fc.T)
        bfc = jnp.zeros((LANE, 1), F32).at[:H_FOX, 0].set(b_forget[l])
        gmat, eps, gain = _proj_epilogue_tables(qnorm_diff[l], knorm_diff[l], qnorm_fox[l], knorm_fox[l])
        qk, vt, dtok = _proj(xs, attn_norm[l][None, :], w_qk, w_v, wfc_t, bfc, gmat, eps, gain,
                             tm=tl["proj_tm"], tn=tl["proj_tn"])

        lam_init = 0.8 - 0.6 * math.exp(-0.3 * l)
        lam = (jnp.exp(jnp.sum(lam_q1[l] * lam_k1[l])) - jnp.exp(jnp.sum(lam_q2[l] * lam_k2[l]))
               + lam_init).reshape(1).astype(F32)
        oa = _diff(qk, vt, slopes, lam, (subln_diff[l] * (1.0 - lam_init))[:, None], t=t)
        ob = _sb(qk, vt, onorm_sb[l][:, None], t=t, tc=min(tl["sb_tc"], t))
        oc = _fox(qk, vt, dtok, onorm_fox[l][:, None], t=t)

        wo = w_out[l].astype(BF16)
        na, nb = H_DIFF * LANE, (H_DIFF + H_SB) * LANE
        h, hn, comb = _outproj(xs, oa, ob, oc, wo[:na], wo[na:nb], wo[nb:], ffn_norm[l][None, :],
                               wrh, wrl, rb, tm=tl["out_tm"])
        xs = _moe(h, hn, comb, w_gate[l].astype(BF16), w_up[l].astype(BF16), w_down[l].astype(BF16),
                  tm=tl["moe_tm"])
    return xs[None]
```

```python
import functools
import math

import jax
import jax.numpy as jnp
from jax import lax
from jax.experimental import pallas as pl
from jax.experimental.pallas import tpu as pltpu

F32 = jnp.float32
BF16 = jnp.bfloat16

D_MODEL = 2048
HEAD_DIM = 128
LANE = 128
BF16_ROWS = 16
N_HEADS = 16
H_SB = N_HEADS // 3
H_FOX = N_HEADS // 3
H_DIFF = N_HEADS - H_SB - H_FOX
DIFF_HALF = HEAD_DIM // 2
CHUNK = 64
N_EXPERTS = 16
N_GROUPS = 4
EXPERTS_PER_GROUP = N_EXPERTS // N_GROUPS
D_EXPERT = D_MODEL // 4
EPS = 1e-6
LOG2E = 1.4426950408889634

QA, KA = 0, H_DIFF
QB, KB = 2 * H_DIFF, 2 * H_DIFF + H_SB
QC, KC = 2 * H_DIFF + 2 * H_SB, 2 * H_DIFF + 2 * H_SB + H_FOX
N_QK = 2 * N_HEADS * HEAD_DIM
VA, VB, VC = 0, H_DIFF, H_DIFF + H_SB
N_V = N_HEADS * HEAD_DIM

VMEM_LIMIT = 48 * 1024 * 1024

_NT = (((1,), (1,)), ((), ()))


def _dot(a, b):
    return jnp.dot(a, b, preferred_element_type=F32)


def _dot_nt(a, b):
    return lax.dot_general(a, b, _NT, preferred_element_type=F32)


def _log_sigmoid(x):
    return jnp.minimum(x, 0.0) - jnp.log1p(jnp.exp(-jnp.abs(x)))


def _split_bf16(x):
    hi = x.astype(BF16)
    lo = (x - hi.astype(F32)).astype(BF16)
    return hi, lo


def _split3_bf16(x):
    hi = x.astype(BF16)
    r = x - hi.astype(F32)
    mid = r.astype(BF16)
    lo = (r - mid.astype(F32)).astype(BF16)
    return hi, mid, lo


def _proj_kernel(x_ref, g_ref, wqk_ref, wvt_ref, wfc_ref, bfc_ref, gmat_ref, eps_ref, gain_ref,
                 qk_ref, vt_ref, dtok_ref, carry_scr, *, tm, tn):
    x = x_ref[...]
    ms = jnp.mean(x * x, axis=-1, keepdims=True)
    xn = (x * lax.rsqrt(ms + EPS) * g_ref[...]).astype(BF16)

    @pl.when(pl.program_id(0) == 0)
    def _():
        carry_scr[...] = jnp.zeros_like(carry_scr)

    lf = _log_sigmoid(_dot_nt(wfc_ref[...], xn) + bfc_ref[...])
    hi, lo = _split_bf16(lf)
    r = lax.broadcasted_iota(jnp.int32, (tm, tm), 0)
    c = lax.broadcasted_iota(jnp.int32, (tm, tm), 1)
    tri = (r <= c).astype(BF16)
    cum = _dot(hi, tri) + _dot(lo, tri) + carry_scr[...]
    carry_scr[...] = cum[:, tm - 1:tm]
    dtok_ref[...] = (cum * LOG2E).T

    def project(j):
        return _dot(xn, wqk_ref[:, pl.ds(pl.multiple_of(j * tn, tn), tn)])

    def epilogue(j, acc):
        for c in range(tn // LANE):
            cols = pl.ds(pl.multiple_of(j * tn + c * LANE, LANE), LANE)
            y = acc[:, c * LANE:(c + 1) * LANE]
            ss = _dot((y * y).astype(BF16), gmat_ref[j * (tn // LANE) + c])
            qk_ref[:, cols] = (y * lax.rsqrt(ss + eps_ref[:, cols]) * gain_ref[:, cols]).astype(BF16)

    def qk_block(j, acc):
        nxt = project(j + 1)
        epilogue(j, acc)
        return nxt

    last = N_QK // tn - 1
    epilogue(last, lax.fori_loop(0, last, qk_block, project(0)))

    def v_block(j, carry):
        rows = pl.ds(pl.multiple_of(j * tn, tn), tn)
        vt_ref[rows, :] = _dot_nt(wvt_ref[rows, :], xn).astype(BF16)
        return carry

    lax.fori_loop(0, N_V // tn, v_block, 0)


def _proj(x, g, wqk, wv_t, wfc_t, bfc, gmat, eps, gain, *, tm, tn):
    s = x.shape[0]
    fixed2 = lambda i: (0, 0)
    resident = pl.Buffered(1)
    return pl.pallas_call(
        functools.partial(_proj_kernel, tm=tm, tn=tn),
        grid=(s // tm,),
        in_specs=[
            pl.BlockSpec((tm, D_MODEL), lambda i: (i, 0)),
            pl.BlockSpec((1, D_MODEL), fixed2),
            pl.BlockSpec((D_MODEL, N_QK), fixed2, pipeline_mode=resident),
            pl.BlockSpec((N_V, D_MODEL), fixed2, pipeline_mode=resident),
            pl.BlockSpec((LANE, D_MODEL), fixed2),
            pl.BlockSpec((LANE, 1), fixed2),
            pl.BlockSpec((N_QK // LANE, LANE, LANE), lambda i: (0, 0, 0), pipeline_mode=resident),
            pl.BlockSpec((1, N_QK), fixed2),
            pl.BlockSpec((1, N_QK), fixed2),
        ],
        out_specs=[
            pl.BlockSpec((tm, N_QK), lambda i: (i, 0)),
            pl.BlockSpec((N_V, tm), lambda i: (0, i)),
            pl.BlockSpec((tm, LANE), lambda i: (i, 0)),
        ],
        out_shape=[
            jax.ShapeDtypeStruct((s, N_QK), BF16),
            jax.ShapeDtypeStruct((N_V, s), BF16),
            jax.ShapeDtypeStruct((s, LANE), F32),
        ],
        scratch_shapes=[pltpu.VMEM((LANE, 1), F32)],
        compiler_params=pltpu.CompilerParams(
            dimension_semantics=("arbitrary",), vmem_limit_bytes=VMEM_LIMIT),
        name="proj",
    )(x, g, wqk, wv_t, wfc_t, bfc, gmat, eps, gain)


def _bias_lanes(split, ones_value, *, split_first):
    hi, mid, lo = split
    lane = lax.broadcasted_iota(jnp.int32, hi.shape, 1)
    s0, c0 = (0, 3) if split_first else (3, 0)
    const = jnp.where((lane >= c0) & (lane < c0 + 3), ones_value, 0.0).astype(BF16)
    return jnp.where(lane == s0, hi, jnp.where(lane == s0 + 1, mid, jnp.where(lane == s0 + 2, lo, const)))


def _query_major(q):
    return q.astype(F32).T.astype(BF16)


def _kq_iotas(t):
    return (lax.broadcasted_iota(jnp.int32, (t, t), 0),
            lax.broadcasted_iota(jnp.int32, (t, t), 1))


def _fold_rows(x, op):
    while x.shape[0] > 8:
        half = x.shape[0] // 2
        x = op(x[:half], x[half:])
    return x


def _softmax_step(s, vt, m_scr, acc_scr):
    m_old = m_scr[...]
    m_new = jnp.maximum(m_old, jnp.max(_fold_rows(s, jnp.maximum), axis=0, keepdims=True))
    alpha = jnp.exp2(m_old - m_new)
    p = jnp.exp2(s - m_new).astype(BF16)
    vt_aug = jnp.concatenate([vt, jnp.ones((BF16_ROWS, vt.shape[1]), BF16)], axis=0)
    acc_scr[...] = alpha * acc_scr[...] + _dot(vt_aug, p)
    m_scr[...] = m_new


def _finish_head(o_t, gcol_ref, o_ref):
    ms = jnp.mean(o_t * o_t, axis=0, keepdims=True)
    o_ref[...] = (o_t * lax.rsqrt(ms + EPS) * gcol_ref[...]).T.astype(BF16)


def _pipelined_blocks(qi, issue_scores, consume):
    issue_scores(0, 0)

    def body(p, carry):
        issue_scores(2 * p + 1, 1)
        consume(2 * p, 0, False)
        issue_scores(2 * p + 2, 0)
        consume(2 * p + 1, 1, False)
        return carry

    lax.fori_loop(0, qi // 2, body, 0)

    @pl.when(qi % 2 == 0)
    def _():
        consume(qi, 0, True)

    @pl.when(qi % 2 == 1)
    def _():
        issue_scores(qi, 1)
        consume(qi - 1, 0, False)
        consume(qi, 1, True)


def _fox_kernel(q_ref, k_ref, vt_ref, d_ref, g_ref, o_ref, kaug_scr, s_scr, m_scr, acc_scr, *, t, nblk):
    h = pl.program_id(0)
    qi = pl.program_id(1)
    pick = (lax.broadcasted_iota(jnp.int32, (LANE, LANE), 0) == h).astype(BF16)

    def head_bias(rows):
        return tuple(_dot(p, pick).astype(BF16) for p in _split3_bf16(d_ref[rows, :]))

    @pl.when(qi == 0)
    def _():
        def build(b, carry):
            rows = pl.ds(pl.multiple_of(b * t, t), t)
            kaug_scr[rows, :LANE] = k_ref[rows, :]
            kaug_scr[rows, LANE:] = _bias_lanes(head_bias(rows), 1.0, split_first=True)
            return carry
        lax.fori_loop(0, nblk, build, 0)

    qrows = pl.ds(pl.multiple_of(qi * t, t), t)
    q_aug = _query_major(
        jnp.concatenate([q_ref[...], _bias_lanes(head_bias(qrows), -1.0, split_first=False)], axis=1))
    m_scr[...] = jnp.full_like(m_scr, -jnp.inf)
    acc_scr[...] = jnp.zeros_like(acc_scr)

    def issue_scores(j, slot):
        s_scr[slot] = _dot(kaug_scr[pl.ds(pl.multiple_of(j * t, t), t), :], q_aug)

    def consume(j, slot, masked):
        s = s_scr[slot]
        if masked:
            kpos, qpos = _kq_iotas(t)
            s = jnp.where(kpos <= qpos, s, -jnp.inf)
        _softmax_step(s, vt_ref[:, pl.ds(pl.multiple_of(j * t, t), t)], m_scr, acc_scr)

    _pipelined_blocks(qi, issue_scores, consume)
    acc = acc_scr[...]
    _finish_head(acc[:HEAD_DIM] / acc[HEAD_DIM:HEAD_DIM + 1], g_ref, o_ref)


def _diff_kernel(slope_ref, lam_ref, q_ref, k_ref, vt_ref, g_ref, o_ref,
                 kaug_scr, s_scr, m1_scr, a1_scr, m2_scr, a2_scr, *, t, nblk):
    h = pl.program_id(0)
    qi = pl.program_id(1)
    slope = slope_ref[h]
    lam = lam_ref[0]

    def pos_split(block, sign):
        pos = (block * t + lax.broadcasted_iota(jnp.int32, (t, LANE), 0)).astype(F32)
        return _split3_bf16(sign * slope * pos)

    @pl.when(qi == 0)
    def _():
        def build(b, carry):
            rows = pl.ds(pl.multiple_of(b * t, t), t)
            kaug_scr[rows, :LANE] = k_ref[rows, :]
            kaug_scr[rows, LANE:] = _bias_lanes(pos_split(b, 1.0), 1.0, split_first=True)
            return carry
        lax.fori_loop(0, nblk, build, 0)

    q = q_ref[...]
    lane = lax.broadcasted_iota(jnp.int32, q.shape, 1)
    zero = jnp.zeros_like(q)
    q_bias = _bias_lanes(pos_split(qi, -1.0), 1.0, split_first=False)
    q1_aug = _query_major(jnp.concatenate([jnp.where(lane < DIFF_HALF, q, zero), q_bias], axis=1))
    q2_aug = _query_major(jnp.concatenate([jnp.where(lane >= DIFF_HALF, q, zero), q_bias], axis=1))
    for m_scr, a_scr in ((m1_scr, a1_scr), (m2_scr, a2_scr)):
        m_scr[...] = jnp.full_like(m_scr, -jnp.inf)
        a_scr[...] = jnp.zeros_like(a_scr)

    def issue_scores(j, slot):
        kb = kaug_scr[pl.ds(pl.multiple_of(j * t, t), t), :]
        s_scr[slot, 0] = _dot(kb, q1_aug)
        s_scr[slot, 1] = _dot(kb, q2_aug)

    def consume(j, slot, masked):
        vt = vt_ref[:, pl.ds(pl.multiple_of(j * t, t), t)]
        for idx, (m_scr, a_scr) in enumerate(((m1_scr, a1_scr), (m2_scr, a2_scr))):
            s = s_scr[slot, idx]
            if masked:
                kpos, qpos = _kq_iotas(t)
                ahead = (kpos - qpos).astype(F32)
                s = s - jnp.where(ahead > 0.0, (2.0 * slope) * ahead, 0.0)
                s = jnp.where((kpos // CHUNK) <= (qpos // CHUNK), s, -jnp.inf)
            _softmax_step(s, vt, m_scr, a_scr)

    _pipelined_blocks(qi, issue_scores, consume)
    a1 = a1_scr[...]
    a2 = a2_scr[...]
    o_t = (a1[:HEAD_DIM] / a1[HEAD_DIM:HEAD_DIM + 1]
           - lam * (a2[:HEAD_DIM] / a2[HEAD_DIM:HEAD_DIM + 1]))
    _finish_head(o_t, g_ref, o_ref)


def _sb_kernel(q_ref, k_ref, vt_ref, g_ref, o_ref, s_scr, run_scr, acc_scr, *, t, tc):
    qi = pl.program_id(1)
    q = _query_major(q_ref[...])
    run_scr[...] = jnp.zeros_like(run_scr)
    acc_scr[...] = jnp.zeros_like(acc_scr)
    ur = lax.broadcasted_iota(jnp.int32, (tc, tc), 0)
    uc = lax.broadcasted_iota(jnp.int32, (tc, tc), 1)
    later = jnp.concatenate([(uc > ur).astype(BF16), jnp.ones((BF16_ROWS, tc), BF16)], axis=0)
    nchunk = t // tc

    def issue_scores(j, slot):
        s_scr[slot] = _dot(k_ref[pl.ds(pl.multiple_of(j * t, t), t), :], q)

    def consume(j, slot, masked):
        z = s_scr[slot]
        log_beta = jnp.minimum(z, 0.0) - jnp.log2(1.0 + jnp.exp2(-jnp.abs(z)))
        log_keep = log_beta - z
        if masked:
            kpos, qpos = _kq_iotas(t)
            strict = kpos < qpos
            log_keep = jnp.where(strict, log_keep, 0.0)
        run = run_scr[...]
        acc = acc_scr[...]
        for ci in range(nchunk - 1, -1, -1):
            sl = slice(ci * tc, (ci + 1) * tc)
            sums = _dot(later, log_keep[sl].astype(BF16))
            a = jnp.exp2(log_beta[sl] + sums[:tc])
            if masked:
                a = jnp.where(strict[sl], a, 0.0)
            cols = pl.ds(pl.multiple_of(j * t + ci * tc, tc), tc)
            acc = acc + jnp.exp2(run) * _dot(vt_ref[:, cols], a.astype(BF16))
            run = run + sums[tc:tc + 1]
        acc_scr[...] = acc
        run_scr[...] = run

    issue_scores(qi, 0)
    issue_scores(jnp.maximum(qi - 1, 0), 1)
    consume(qi, 0, True)

    def rest(p, carry):
        issue_scores(jnp.maximum(qi - 2 * p - 2, 0), 0)
        consume(qi - 2 * p - 1, 1, False)
        issue_scores(jnp.maximum(qi - 2 * p - 3, 0), 1)
        consume(qi - 2 * p - 2, 0, False)
        return carry

    lax.fori_loop(0, qi // 2, rest, 0)

    @pl.when(qi % 2 == 1)
    def _():
        consume(0, 1, False)
    _finish_head(acc_scr[...], g_ref, o_ref)


def _attn_specs(s, t, qc, kc, vc):
    return [
        pl.BlockSpec((t, LANE), lambda h, i: (i, qc + h)),
        pl.BlockSpec((s, LANE), lambda h, i: (0, kc + h)),
        pl.BlockSpec((LANE, s), lambda h, i: (vc + h, 0)),
    ]


_ATTN_PARAMS = pltpu.CompilerParams(
    dimension_semantics=("arbitrary", "arbitrary"), vmem_limit_bytes=VMEM_LIMIT)
_GCOL_SPEC = pl.BlockSpec((LANE, 1), lambda h, i: (0, 0))


def _fox(qk, vt, dtok, gcol, *, t):
    s = qk.shape[0]
    return pl.pallas_call(
        functools.partial(_fox_kernel, t=t, nblk=s // t),
        grid=(H_FOX, s // t),
        in_specs=_attn_specs(s, t, QC, KC, VC) + [
            pl.BlockSpec((s, LANE), lambda h, i: (0, 0)),
            _GCOL_SPEC,
        ],
        out_specs=pl.BlockSpec((t, LANE), lambda h, i: (i, h)),
        out_shape=jax.ShapeDtypeStruct((s, H_FOX * LANE), BF16),
        scratch_shapes=[
            pltpu.VMEM((s, 2 * LANE), BF16),
            pltpu.VMEM((2, t, t), F32),
            pltpu.VMEM((1, t), F32),
            pltpu.VMEM((HEAD_DIM + BF16_ROWS, t), F32),
        ],
        compiler_params=_ATTN_PARAMS,
        name="fox_attn",
    )(qk, qk, vt, dtok, gcol)


def _diff(qk, vt, slopes, lam, gcol, *, t):
    s = qk.shape[0]
    row = pltpu.VMEM((1, t), F32)
    acc = pltpu.VMEM((HEAD_DIM + BF16_ROWS, t), F32)
    smem = pl.BlockSpec(memory_space=pltpu.SMEM)
    return pl.pallas_call(
        functools.partial(_diff_kernel, t=t, nblk=s // t),
        grid=(H_DIFF, s // t),
        in_specs=[smem, smem] + _attn_specs(s, t, QA, KA, VA) + [_GCOL_SPEC],
        out_specs=pl.BlockSpec((t, LANE), lambda h, i: (i, h)),
        out_shape=jax.ShapeDtypeStruct((s, H_DIFF * LANE), BF16),
        scratch_shapes=[
            pltpu.VMEM((s, 2 * LANE), BF16),
            pltpu.VMEM((2, 2, t, t), F32),
            row, acc, row, acc,
        ],
        compiler_params=_ATTN_PARAMS,
        name="diff_attn",
    )(slopes, lam, qk, qk, vt, gcol)


def _sb(qk, vt, gcol, *, t, tc):
    s = qk.shape[0]
    return pl.pallas_call(
        functools.partial(_sb_kernel, t=t, tc=tc),
        grid=(H_SB, s // t),
        in_specs=_attn_specs(s, t, QB, KB, VB) + [_GCOL_SPEC],
        out_specs=pl.BlockSpec((t, LANE), lambda h, i: (i, h)),
        out_shape=jax.ShapeDtypeStruct((s, H_SB * LANE), BF16),
        scratch_shapes=[
            pltpu.VMEM((2, t, t), F32),
            pltpu.VMEM((1, t), F32),
            pltpu.VMEM((HEAD_DIM, t), F32),
        ],
        compiler_params=_ATTN_PARAMS,
        name="sb_attn",
    )(qk, qk, vt, gcol)


def _route(logits_t, bias):
    score = [jax.nn.sigmoid(logits_t[e:e + 1, :]) for e in range(N_EXPERTS)]
    sel = [score[e] + bias[e:e + 1, :] for e in range(N_EXPERTS)]
    gscore = []
    for g in range(N_GROUPS):
        mem = sel[g * EXPERTS_PER_GROUP:(g + 1) * EXPERTS_PER_GROUP]
        best = None
        for a in range(EXPERTS_PER_GROUP):
            for b in range(a + 1, EXPERTS_PER_GROUP):
                pair = mem[a] + mem[b]
                best = pair if best is None else jnp.maximum(best, pair)
        gscore.append(best)
    gmax = functools.reduce(jnp.maximum, gscore)
    taken = None
    in_group = []
    for g in range(N_GROUPS):
        hit = gscore[g] == gmax
        if taken is not None:
            hit = hit & ~taken
        taken = hit if taken is None else taken | hit
        in_group.append(hit)
    masked = [jnp.where(in_group[e // EXPERTS_PER_GROUP], sel[e], -jnp.inf) for e in range(N_EXPERTS)]

    def first_max(vals):
        top = functools.reduce(jnp.maximum, vals)
        seen = None
        picks = []
        for v in vals:
            hit = v == top
            if seen is not None:
                hit = hit & ~seen
            seen = hit if seen is None else seen | hit
            picks.append(hit)
        return picks

    pick1 = first_max(masked)
    rest = [jnp.where(pick1[e], -jnp.inf, masked[e]) for e in range(N_EXPERTS)]
    pick2 = first_max(rest)
    chosen = [pick1[e] | pick2[e] for e in range(N_EXPERTS)]
    gate = [jnp.where(chosen[e], score[e], 0.0) for e in range(N_EXPERTS)]
    total = functools.reduce(lambda a, b: a + b, gate)
    return jnp.concatenate([gt / total for gt in gate], axis=0)


def _outproj_kernel(x_ref, oa_ref, ob_ref, oc_ref, wa_ref, wb_ref, wc_ref, g_ref,
                    wrh_ref, wrl_ref, rb_ref, h_ref, hn_ref, comb_ref, *, tm):
    h = x_ref[...] + _dot(oa_ref[...], wa_ref[...]) + _dot(ob_ref[...], wb_ref[...]) \
        + _dot(oc_ref[...], wc_ref[...])
    h_ref[...] = h
    ms = jnp.mean(h * h, axis=-1, keepdims=True)
    hn = h * lax.rsqrt(ms + EPS) * g_ref[...]
    hi, lo = _split_bf16(hn)
    hn_ref[...] = hi
    wrh = wrh_ref[...]
    logits_t = _dot_nt(wrh, hi) + _dot_nt(wrh, lo) + _dot_nt(wrl_ref[...], hi)
    comb = _route(logits_t, rb_ref[...])
    pad = jnp.zeros((LANE - N_EXPERTS, tm), F32)
    comb_ref[...] = jnp.concatenate([comb, pad], axis=0).T


def _outproj(x, oa, ob, oc, wa, wb, wc, g, wrh, wrl, rb, *, tm):
    s = x.shape[0]
    row = lambda i: (i, 0)
    fixed = lambda i: (0, 0)
    return pl.pallas_call(
        functools.partial(_outproj_kernel, tm=tm),
        grid=(s // tm,),
        in_specs=[
            pl.BlockSpec((tm, D_MODEL), row),
            pl.BlockSpec((tm, H_DIFF * LANE), row),
            pl.BlockSpec((tm, H_SB * LANE), row),
            pl.BlockSpec((tm, H_FOX * LANE), row),
            pl.BlockSpec((H_DIFF * LANE, D_MODEL), fixed, pipeline_mode=pl.Buffered(1)),
            pl.BlockSpec((H_SB * LANE, D_MODEL), fixed, pipeline_mode=pl.Buffered(1)),
            pl.BlockSpec((H_FOX * LANE, D_MODEL), fixed, pipeline_mode=pl.Buffered(1)),
            pl.BlockSpec((1, D_MODEL), fixed),
            pl.BlockSpec((N_EXPERTS, D_MODEL), fixed),
            pl.BlockSpec((N_EXPERTS, D_MODEL), fixed),
            pl.BlockSpec((N_EXPERTS, 1), fixed),
        ],
        out_specs=[
            pl.BlockSpec((tm, D_MODEL), row),
            pl.BlockSpec((tm, D_MODEL), row),
            pl.BlockSpec((tm, LANE), row),
        ],
        out_shape=[
            jax.ShapeDtypeStruct((s, D_MODEL), F32),
            jax.ShapeDtypeStruct((s, D_MODEL), BF16),
            jax.ShapeDtypeStruct((s, LANE), F32),
        ],
        compiler_params=pltpu.CompilerParams(
            dimension_semantics=("arbitrary",), vmem_limit_bytes=VMEM_LIMIT),
        name="outproj_router",
    )(x, oa, ob, oc, wa, wb, wc, g, wrh, wrl, rb)


def _moe_kernel(h_ref, hn_ref, comb_ref, wg_ref, wu_ref, wd_ref, o_ref):
    e = pl.program_id(1)

    @pl.when(e == 0)
    def _():
        o_ref[...] = h_ref[...]

    hn = hn_ref[...]
    comb = comb_ref[...]
    lane = lax.broadcasted_iota(jnp.int32, comb.shape, 1)
    weight = jnp.sum(jnp.where(lane == e, comb, 0.0), axis=-1, keepdims=True)
    gate = _dot(hn, wg_ref[...])
    up = _dot(hn, wu_ref[...])
    act = gate * jax.nn.sigmoid(gate) * up * weight
    o_ref[...] += _dot(act.astype(BF16), wd_ref[...])


def _moe(h, hn, comb, wg, wu, wd, *, tm):
    s = h.shape[0]
    row = lambda i, e: (i, 0)
    return pl.pallas_call(
        _moe_kernel,
        grid=(s // tm, N_EXPERTS),
        in_specs=[
            pl.BlockSpec((tm, D_MODEL), row),
            pl.BlockSpec((tm, D_MODEL), row),
            pl.BlockSpec((tm, LANE), row),
            pl.BlockSpec((None, D_MODEL, D_EXPERT), lambda i, e: (e, 0, 0)),
            pl.BlockSpec((None, D_MODEL, D_EXPERT), lambda i, e: (e, 0, 0)),
            pl.BlockSpec((None, D_EXPERT, D_MODEL), lambda i, e: (e, 0, 0)),
        ],
        out_specs=pl.BlockSpec((tm, D_MODEL), row),
        out_shape=jax.ShapeDtypeStruct((s, D_MODEL), F32),
        compiler_params=pltpu.CompilerParams(
            dimension_semantics=("arbitrary", "arbitrary"), vmem_limit_bytes=VMEM_LIMIT),
        name="moe_experts",
    )(h, hn, comb, wg, wu, wd)


def _proj_epilogue_tables(qnorm_diff, knorm_diff, qnorm_fox, knorm_fox):
    ones = jnp.ones((HEAD_DIM,), F32)
    normed = jnp.full((HEAD_DIM,), EPS, F32)
    plain = ones
    half = jnp.arange(HEAD_DIM) // DIFF_HALF
    g_diff = (half[:, None] == half[None, :]).astype(F32) / DIFF_HALF
    g_full = jnp.full((HEAD_DIM, HEAD_DIM), 1.0 / HEAD_DIM, F32)
    g_none = jnp.zeros((HEAD_DIM, HEAD_DIM), F32)
    diff_scale = DIFF_HALF ** -0.5 * LOG2E
    full_scale = HEAD_DIM ** -0.5 * LOG2E
    groups = [
        (H_DIFF, g_diff, normed, jnp.tile(qnorm_diff, 2) * diff_scale),
        (H_DIFF, g_diff, normed, jnp.tile(knorm_diff, 2)),
        (H_SB, g_none, plain, ones * full_scale),
        (H_SB, g_none, plain, ones),
        (H_FOX, g_full, normed, qnorm_fox * full_scale),
        (H_FOX, g_full, normed, knorm_fox),
    ]
    gmat = jnp.concatenate([jnp.broadcast_to(gm, (n, HEAD_DIM, HEAD_DIM)) for n, gm, _, _ in groups])
    eps = jnp.concatenate([jnp.tile(ep, n) for n, _, ep, _ in groups])[None, :]
    gain = jnp.concatenate([jnp.tile(gn, n) for n, _, _, gn in groups])[None, :]
    return gmat.astype(BF16), eps, gain


def _split_in_weights(w):
    wa, wb, wc = H_DIFF * HEAD_DIM, H_SB * HEAD_DIM, H_FOX * HEAD_DIM
    pieces = []
    off = 0
    for width in (wa, wa, wa, wb, wb, wb, wc, wc, wc):
        pieces.append(w[:, off:off + width])
        off += width
    qa, ka, va, qb, kb, vb, qc, kc, vc = pieces
    return (jnp.concatenate([qa, ka, qb, kb, qc, kc], axis=1),
            jnp.concatenate([va, vb, vc], axis=1), w[:, off:])


def _tiles(s):
    return dict(proj_tm=min(256, s), proj_tn=512, attn_t=min(512, s), sb_tc=256,
                out_tm=min(512, s), moe_tm=min(512, s))


def kernel(x, attn_norm, w_in, b_forget, qnorm_diff, knorm_diff, lam_q1, lam_k1, lam_q2, lam_k2,
           subln_diff, onorm_sb, qnorm_fox, knorm_fox, onorm_fox, w_out, ffn_norm,
           w_router, router_bias, w_gate, w_up, w_down):
    b, s, d = x.shape
    assert b == 1 and d == D_MODEL
    depth = w_in.shape[0]
    tl = _tiles(s)
    t = tl["attn_t"]
    slopes = jnp.exp2(-8.0 * jnp.arange(1, H_DIFF + 1, dtype=F32) / H_DIFF) * LOG2E
    wr_t = w_router.T.astype(F32)
    wrh = wr_t.astype(BF16)
    wrl = (wr_t - wrh.astype(F32)).astype(BF16)
    rb = router_bias.astype(F32)[:, None]
    xs = x[0]
    w_in_b = lax.optimization_barrier(w_in.astype(BF16))
    for l in range(depth):
        w_qk, w_v, w_fc = _split_in_weights(w_in_b[l])
        wfc_t = jnp.zeros((LANE, D_MODEL), BF16).at[:H_FOX].set(w_fc.T)
        bfc = jnp.zeros((LANE, 1), F32).at[:H_FOX, 0].set(b_forget[l])
        gmat, eps, gain = _proj_epilogue_tables(qnorm_diff[l], knorm_diff[l], qnorm_fox[l], knorm_fox[l])
        qk, vt, dtok = _proj(xs, attn_norm[l][None, :], w_qk, w_v.T, wfc_t, bfc, gmat, eps, gain,
                             tm=tl["proj_tm"], tn=tl["proj_tn"])

        lam_init = 0.8 - 0.6 * math.exp(-0.3 * l)
        lam = (jnp.exp(jnp.sum(lam_q1[l] * lam_k1[l])) - jnp.exp(jnp.sum(lam_q2[l] * lam_k2[l]))
               + lam_init).reshape(1).astype(F32)
        oa = _diff(qk, vt, slopes, lam, (subln_diff[l] * (1.0 - lam_init))[:, None], t=t)
        ob = _sb(qk, vt, onorm_sb[l][:, None], t=t, tc=min(tl["sb_tc"], t))
        oc = _fox(qk, vt, dtok, onorm_fox[l][:, None], t=t)

        wo = w_out[l].astype(BF16)
        na, nb = H_DIFF * LANE, (H_DIFF + H_SB) * LANE
        h, hn, comb = _outproj(xs, oa, ob, oc, wo[:na], wo[na:nb], wo[nb:], ffn_norm[l][None, :],
                               wrh, wrl, rb, tm=tl["out_tm"])
        xs = _moe(h, hn, comb, w_gate[l].astype(BF16), w_up[l].astype(BF16), w_down[l].astype(BF16),
                  tm=tl["moe_tm"])
    return xs[None]
```
